```python
import math
import jax
import jax.numpy as jnp
from jax import lax
import numpy as np

D_MODEL = 2048
BATCH = 4
SEQ = 2048
DEPTH = 2

HEAD_DIM = 128
ATTN_WIDTH = D_MODEL // 2
ATTN_HEADS = ATTN_WIDTH // HEAD_DIM
MOBA_BLOCK = 256
MOBA_TOPK = 3
Q_CHUNK = 64
ROPE_THETA = 10000.0
SSM_WIDTH = D_MODEL // 4
SSM_HEAD_DIM = 64
SSM_HEADS = SSM_WIDTH // SSM_HEAD_DIM
SSM_GROUPS = 2
SSM_STATE = 128
SSM_CONV = 4
SSM_CHUNK = 128
SSM_CONV_DIM = SSM_WIDTH + 2 * SSM_GROUPS * SSM_STATE
CONV_WIDTH = D_MODEL // 4
CONV_SIZE = 31
MIX_WIDTH = ATTN_WIDTH + SSM_WIDTH + CONV_WIDTH
IN_SPLITS = (ATTN_WIDTH, ATTN_WIDTH, ATTN_WIDTH, SSM_WIDTH, SSM_CONV_DIM, SSM_HEADS, 2 * CONV_WIDTH)
IN_DIM = sum(IN_SPLITS)
D_FF = 7 * D_MODEL // 2
N_EXPERTS = 8
TOP_K = 2
N_DENSE = (DEPTH + 1) // 2
N_MOE = DEPTH // 2
EPS = 1e-5

kernel_name = 'hybrid_moba_ssd_conformer_moe'


def rms_norm(x, w):
    xf = x.astype(jnp.float32)
    y = xf * lax.rsqrt(jnp.mean(xf * xf, axis=-1, keepdims=True) + EPS)
    return (y * w.astype(jnp.float32)).astype(x.dtype)


def rope_tables(seqlen):
    pos = jnp.arange(seqlen, dtype=jnp.float32)
    inv_freq = 1.0 / (ROPE_THETA ** (jnp.arange(0, HEAD_DIM, 2, dtype=jnp.float32) / HEAD_DIM))
    ang = pos[:, None] * inv_freq[None, :]
    return jnp.cos(ang), jnp.sin(ang)


def apply_rope(x, cos, sin):
    xf = x.astype(jnp.float32)
    half = HEAD_DIM // 2
    x1, x2 = xf[..., :half], xf[..., half:]
    out = jnp.concatenate([x1 * cos - x2 * sin, x2 * cos + x1 * sin], axis=-1)
    return out.astype(x.dtype)


def causal_depthwise_conv(x, w, b):
    k = w.shape[0]
    y = lax.conv_general_dilated(
        x, w[:, None, :].astype(x.dtype), window_strides=(1,), padding=((k - 1, 0),),
        dimension_numbers=('NWC', 'WIO', 'NWC'), feature_group_count=x.shape[-1])
    return y + b.astype(x.dtype)


def moba_attention(q, k, v):
    bsz, nh, seqlen, hd = q.shape
    s_pad = -(-seqlen // MOBA_BLOCK) * MOBA_BLOCK
    pad = ((0, 0), (0, 0), (0, s_pad - seqlen), (0, 0))
    q = jnp.pad(q, pad)
    k = jnp.pad(k, pad)
    v = jnp.pad(v, pad)
    nb = s_pad // MOBA_BLOCK
    topk = min(MOBA_TOPK, nb)
    k_blocks = k.reshape(bsz, nh, nb, MOBA_BLOCK, hd)
    v_blocks = v.reshape(bsz, nh, nb, MOBA_BLOCK, hd)
    k_mean = jnp.mean(k_blocks.astype(jnp.float32), axis=3)
    scale = HEAD_DIM ** -0.5
    bi = jnp.arange(bsz)[:, None, None, None]
    hi = jnp.arange(nh)[None, :, None, None]

    def one_chunk(i):
        start = i * Q_CHUNK
        blk = start // MOBA_BLOCK
        qc = lax.dynamic_slice_in_dim(q, start, Q_CHUNK, axis=2)
        k_own = lax.dynamic_slice_in_dim(k, blk * MOBA_BLOCK, MOBA_BLOCK, axis=2)
        v_own = lax.dynamic_slice_in_dim(v, blk * MOBA_BLOCK, MOBA_BLOCK, axis=2)
        gate = jnp.einsum('bhqd,bhnd->bhqn', qc.astype(jnp.float32), k_mean)
        gate = jnp.where(jnp.arange(nb) < blk, gate, -jnp.inf)
        _, idx = lax.top_k(gate, topk)
        sel_valid = idx < blk
        k_sel = k_blocks[bi, hi, idx]
        v_sel = v_blocks[bi, hi, idx]
        s_own = jnp.einsum('bhqd,bhkd->bhqk', qc, k_own).astype(jnp.float32) * scale
        qpos = start + jnp.arange(Q_CHUNK)
        kpos = blk * MOBA_BLOCK + jnp.arange(MOBA_BLOCK)
        s_own = jnp.where(kpos[None, :] <= qpos[:, None], s_own, -jnp.inf)
        s_sel = jnp.einsum('bhqd,bhqjkd->bhqjk', qc, k_sel).astype(jnp.float32) * scale
        s_sel = jnp.where(sel_valid[..., None], s_sel, -jnp.inf)
        s = jnp.concatenate([s_own, s_sel.reshape(bsz, nh, Q_CHUNK, topk * MOBA_BLOCK)], axis=-1)
        p = jax.nn.softmax(s, axis=-1).astype(v.dtype)
        p_own = p[..., :MOBA_BLOCK]
        p_sel = p[..., MOBA_BLOCK:].reshape(bsz, nh, Q_CHUNK, topk, MOBA_BLOCK)
        return (jnp.einsum('bhqk,bhkd->bhqd', p_own, v_own)
                + jnp.einsum('bhqjk,bhqjkd->bhqd', p_sel, v_sel))

    out = lax.map(one_chunk, jnp.arange(s_pad // Q_CHUNK))
    out = out.transpose(1, 2, 0, 3, 4).reshape(bsz, nh, s_pad, hd)
    return out[:, :, :seqlen]


def ssd_chunked(x, dt, a, b_in, c_in):
    bsz, seqlen, nh, hp = x.shape
    ng, ns = b_in.shape[2], b_in.shape[3]
    hg = nh // ng
    nc = seqlen // SSM_CHUNK
    cl = SSM_CHUNK
    xr = (x * dt[..., None]).reshape(bsz, nc, cl, ng, hg, hp)
    la = (dt * a).reshape(bsz, nc, cl, ng, hg)
    la_cum = jnp.cumsum(la, axis=2)
    br = b_in.reshape(bsz, nc, cl, ng, ns)
    cr = c_in.reshape(bsz, nc, cl, ng, ns)
    causal = jnp.tril(jnp.ones((cl, cl), dtype=bool))
    seg = la_cum[:, :, :, None] - la_cum[:, :, None, :]
    decay = jnp.exp(jnp.where(causal[None, None, :, :, None, None], seg, -jnp.inf))
    cb = jnp.einsum('bctgn,bcsgn->bctsg', cr, br)
    y_diag = jnp.einsum('bctsgh,bcsghp->bctghp', cb[..., None] * decay, xr)
    decay_to_end = jnp.exp(la_cum[:, :, -1:] - la_cum)
    chunk_states = jnp.einsum('bcsgn,bcsghp->bcghpn', br, xr * decay_to_end[..., None])
    chunk_decay = jnp.exp(la_cum[:, :, -1])

    def step(state, inp):
        st, dec = inp
        return state * dec[..., None, None] + st, state

    init = jnp.zeros((bsz, ng, hg, hp, ns), x.dtype)
    _, prev = lax.scan(step, init, (jnp.moveaxis(chunk_states, 1, 0), jnp.moveaxis(chunk_decay, 1, 0)))
    prev = jnp.moveaxis(prev, 0, 1)
    y_off = jnp.einsum('bctgn,bcghpn->bctghp', cr, prev) * jnp.exp(la_cum)[..., None]
    return (y_diag + y_off).reshape(bsz, seqlen, nh, hp)


def mamba2_mixer(z, xbc, dt_raw, conv_w, conv_b, dt_bias, a_log, d_skip, norm_w):
    bsz, seqlen, _ = z.shape
    xbc = jax.nn.silu(causal_depthwise_conv(xbc, conv_w, conv_b))
    gn = SSM_GROUPS * SSM_STATE
    xs = xbc[..., :SSM_WIDTH].astype(jnp.float32).reshape(bsz, seqlen, SSM_HEADS, SSM_HEAD_DIM)
    bm = xbc[..., SSM_WIDTH:SSM_WIDTH + gn].astype(jnp.float32).reshape(bsz, seqlen, SSM_GROUPS, SSM_STATE)
    cm = xbc[..., SSM_WIDTH + gn:].astype(jnp.float32).reshape(bsz, seqlen, SSM_GROUPS, SSM_STATE)
    dt = jax.nn.softplus(dt_raw.astype(jnp.float32) + dt_bias.astype(jnp.float32))
    a = -jnp.exp(a_log.astype(jnp.float32))
    y = ssd_chunked(xs, dt, a, bm, cm) + d_skip.astype(jnp.float32)[:, None] * xs
    y = y.reshape(bsz, seqlen, SSM_WIDTH) * jax.nn.silu(z.astype(jnp.float32))
    yg = y.reshape(bsz, seqlen, SSM_GROUPS, SSM_WIDTH // SSM_GROUPS)
    yg = yg * lax.rsqrt(jnp.mean(yg * yg, axis=-1, keepdims=True) + EPS)
    y = yg.reshape(bsz, seqlen, SSM_WIDTH) * norm_w.astype(jnp.float32)
    return y.astype(z.dtype)


def conformer_conv(u, conv_w, conv_b, ln_w, ln_b):
    a, g = u[..., :CONV_WIDTH], u[..., CONV_WIDTH:]
    hcv = causal_depthwise_conv(a * jax.nn.sigmoid(g), conv_w, conv_b)
    hf = hcv.astype(jnp.float32)
    mu = jnp.mean(hf, axis=-1, keepdims=True)
    var = jnp.mean(jnp.square(hf - mu), axis=-1, keepdims=True)
    hf = (hf - mu) * lax.rsqrt(var + EPS) * ln_w.astype(jnp.float32) + ln_b.astype(jnp.float32)
    return jax.nn.silu(hf).astype(u.dtype)


def swiglu(h, w_gate, w_up, w_down):
    g = jnp.einsum('bsd,df->bsf', h, w_gate)
    u = jnp.einsum('bsd,df->bsf', h, w_up)
    return jnp.einsum('bsf,fd->bsd', jax.nn.silu(g) * u, w_down)


def moe_swiglu(h, w_router, w_gate, w_up, w_down):
    logits = jnp.einsum('bsd,de->bse', h, w_router).astype(jnp.float32)
    top_vals, top_idx = lax.top_k(logits, TOP_K)
    gates = jax.nn.softmax(top_vals, axis=-1)
    combine = jnp.sum(jax.nn.one_hot(top_idx, N_EXPERTS, dtype=jnp.float32) * gates[..., None], axis=-2)
    out = jnp.zeros_like(h)
    for e in range(N_EXPERTS):
        out = out + combine[..., e:e + 1].astype(h.dtype) * swiglu(h, w_gate[e], w_up[e], w_down[e])
    return out


def setup_inputs(seed: int = 0) -> dict:
    key = jax.random.key(seed)
    ks = jax.random.split(key, 24)
    f32 = jnp.float32
    L = DEPTH

    def nrm(k, shape, scale):
        return jax.random.normal(k, shape, f32) * scale

    dt0 = jnp.exp(jax.random.uniform(ks[5], (L, SSM_HEADS), f32, math.log(1e-3), math.log(1e-1)))
    return {
        'x': nrm(ks[0], (BATCH, SEQ, D_MODEL), 1.0),
        'norm_mix': 1.0 + nrm(ks[1], (L, D_MODEL), 0.01),
        'w_in': nrm(ks[2], (L, D_MODEL, IN_DIM), D_MODEL ** -0.5),
        'ssm_conv_w': nrm(ks[3], (L, SSM_CONV, SSM_CONV_DIM), SSM_CONV ** -0.5),
        'ssm_conv_b': nrm(ks[4], (L, SSM_CONV_DIM), 0.01),
        'ssm_dt_bias': dt0 + jnp.log(-jnp.expm1(-dt0)),
        'ssm_a_log': jnp.log(jax.random.uniform(ks[6], (L, SSM_HEADS), f32, 1.0, 16.0)),
        'ssm_d': 1.0 + nrm(ks[7], (L, SSM_HEADS), 0.1),
        'ssm_norm_w': 1.0 + nrm(ks[8], (L, SSM_WIDTH), 0.01),
        'cf_conv_w': nrm(ks[9], (L, CONV_SIZE, CONV_WIDTH), CONV_SIZE ** -0.5),
        'cf_conv_b': nrm(ks[10], (L, CONV_WIDTH), 0.01),
        'cf_ln_w': 1.0 + nrm(ks[11], (L, CONV_WIDTH), 0.01),
        'cf_ln_b': nrm(ks[12], (L, CONV_WIDTH), 0.01),
        'w_out': nrm(ks[13], (L, MIX_WIDTH, D_MODEL), MIX_WIDTH ** -0.5),
        'norm_ffn': 1.0 + nrm(ks[14], (L, D_MODEL), 0.01),
        'ffn_w_gate': nrm(ks[15], (N_DENSE, D_MODEL, D_FF), D_MODEL ** -0.5),
        'ffn_w_up': nrm(ks[16], (N_DENSE, D_MODEL, D_FF), D_MODEL ** -0.5),
        'ffn_w_down': nrm(ks[17], (N_DENSE, D_FF, D_MODEL), D_FF ** -0.5),
        'moe_router': nrm(ks[18], (N_MOE, D_MODEL, N_EXPERTS), D_MODEL ** -0.5),
        'moe_w_gate': nrm(ks[19], (N_MOE, N_EXPERTS, D_MODEL, D_FF), D_MODEL ** -0.5),
        'moe_w_up': nrm(ks[20], (N_MOE, N_EXPERTS, D_MODEL, D_FF), D_MODEL ** -0.5),
        'moe_w_down': nrm(ks[21], (N_MOE, N_EXPERTS, D_FF, D_MODEL), D_FF ** -0.5),
        'norm_final': 1.0 + nrm(ks[22], (D_MODEL,), 0.01),
    }


def reference(x, norm_mix, w_in, ssm_conv_w, ssm_conv_b, ssm_dt_bias, ssm_a_log, ssm_d, ssm_norm_w,
              cf_conv_w, cf_conv_b, cf_ln_w, cf_ln_b, w_out, norm_ffn,
              ffn_w_gate, ffn_w_up, ffn_w_down, moe_router, moe_w_gate, moe_w_up, moe_w_down,
              norm_final):
    bsz, seqlen, _ = x.shape
    cos, sin = rope_tables(seqlen)
    split_at = np.cumsum(IN_SPLITS)[:-1].tolist()

    def to_heads(t):
        return t.reshape(bsz, seqlen, ATTN_HEADS, HEAD_DIM).transpose(0, 2, 1, 3)

    h = x
    for layer in range(DEPTH):
        hn = rms_norm(h, norm_mix[layer])
        proj = jnp.einsum('bsd,de->bse', hn, w_in[layer])
        q, k, v, z, xbc, dt_raw, glu_in = jnp.split(proj, split_at, axis=-1)
        q = apply_rope(to_heads(q), cos, sin)
        k = apply_rope(to_heads(k), cos, sin)
        attn = moba_attention(q, k, to_heads(v))
        attn = attn.transpose(0, 2, 1, 3).reshape(bsz, seqlen, ATTN_WIDTH)
        ssm = mamba2_mixer(z, xbc, dt_raw, ssm_conv_w[layer], ssm_conv_b[layer], ssm_dt_bias[layer],
                           ssm_a_log[layer], ssm_d[layer], ssm_norm_w[layer])
        conv = conformer_conv(glu_in, cf_conv_w[layer], cf_conv_b[layer], cf_ln_w[layer], cf_ln_b[layer])
        mixed = jnp.concatenate([attn, ssm, conv], axis=-1)
        h = h + jnp.einsum('bsm,md->bsd', mixed, w_out[layer])
        hn = rms_norm(h, norm_ffn[layer])
        if layer % 2 == 0:
            i = layer // 2
            h = h + swiglu(hn, ffn_w_gate[i], ffn_w_up[i], ffn_w_down[i])
        else:
            i = layer // 2
            h = h + moe_swiglu(hn, moe_router[i], moe_w_gate[i], moe_w_up[i], moe_w_down[i])
    return rms_norm(h, norm_final)
```

```python
import functools
import math

import jax
import jax.numpy as jnp
from jax import lax
from jax.experimental import pallas as pl
from jax.experimental.pallas import tpu as pltpu

F32 = jnp.float32
BF16 = jnp.bfloat16

D_MODEL = 2048
HEAD_DIM = 128
ATTN_WIDTH = 1024
ATTN_HEADS = 8
MOBA_BLOCK = 256
MOBA_TOPK = 3
ROPE_THETA = 10000.0
SSM_WIDTH = 512
SSM_HEAD_DIM = 64
SSM_HEADS = 8
SSM_GROUPS = 2
SSM_STATE = 128
SSM_CONV = 4
SSM_CHUNK = 128
SSM_CONV_DIM = 1024
CONV_WIDTH = 512
CONV_SIZE = 31
D_FF = 7168
N_EXPERTS = 8
EPS = 1e-5

LANES = 128
NEG = -1e30

COL_Q, COL_K, COL_V = 0, 1024, 2048
COL_XBC, COL_GLU, COL_Z, COL_DT = 3072, 4096, 5120, 5632
IN_PAD = 5760

VMEM_LIMIT = 56 * 1024 * 1024


def _params(*sem):
    return pltpu.CompilerParams(dimension_semantics=sem, vmem_limit_bytes=VMEM_LIMIT)


def _dot(a, b):
    return jnp.dot(a, b, preferred_element_type=F32)


def _dot_nt(a, b):
    return lax.dot_general(a, b, (((1,), (1,)), ((), ())), preferred_element_type=F32)


def _dot_hi(a, b):
    return jnp.dot(a, b, preferred_element_type=F32, precision=lax.Precision.HIGHEST)


def _dot_nt_hi(a, b):
    return lax.dot_general(a, b, (((1,), (1,)), ((), ())), preferred_element_type=F32,
                           precision=lax.Precision.HIGHEST)


def _rms(x, w):
    return x * lax.rsqrt(jnp.mean(x * x, axis=-1, keepdims=True) + EPS) * w


def _silu(x):
    return x * jax.nn.sigmoid(x)


def _inproj_kernel(x_ref, nw_ref, w_ref, o_ref, xn_ref):
    @pl.when(pl.program_id(1) == 0)
    def _():
        xn_ref[...] = _rms(x_ref[...], nw_ref[...]).astype(BF16)

    o_ref[...] = _dot(xn_ref[...], w_ref[...])


def _inproj(h, nw, w, tm=512, tn=1152):
    t = h.shape[0]
    return pl.pallas_call(
        _inproj_kernel,
        grid=(t // tm, IN_PAD // tn),
        in_specs=[pl.BlockSpec((tm, D_MODEL), lambda i, j: (i, 0)),
                  pl.BlockSpec((1, D_MODEL), lambda i, j: (0, 0)),
                  pl.BlockSpec((D_MODEL, tn), lambda i, j: (0, j))],
        out_specs=pl.BlockSpec((tm, tn), lambda i, j: (i, j)),
        out_shape=jax.ShapeDtypeStruct((t, IN_PAD), F32),
        scratch_shapes=[pltpu.VMEM((tm, D_MODEL), BF16)],
        compiler_params=_params("parallel", "arbitrary"),
        name="inproj",
    )(h, nw, w)


def _rope(x, cosf, sinf):
    return x * cosf + pltpu.roll(x, HEAD_DIM // 2, 1) * sinf


def _moba_kernel(q_ref, k_ref, v_ref, cos_ref, sin_ref, o_ref, kr_ref, km_ref, *, nb):
    i = pl.program_id(2)
    blk = MOBA_BLOCK

    @pl.when(i == 0)
    def _():
        km_ref[...] = jnp.zeros_like(km_ref)
        for j in range(nb):
            sl = slice(j * blk, (j + 1) * blk)
            kr = _rope(k_ref[sl, :], cos_ref[sl, :], sin_ref[sl, :])
            kr_ref[sl, :] = kr.astype(BF16)
            km_ref[j:j + 1, :] = jnp.mean(kr, axis=0, keepdims=True)

    row0 = pl.multiple_of(i * blk, blk)
    q = _rope(q_ref[...], cos_ref[pl.ds(row0, blk), :], sin_ref[pl.ds(row0, blk), :])
    gate = _dot_nt_hi(q, km_ref[...])
    lane = lax.broadcasted_iota(jnp.int32, gate.shape, 1)
    gate = jnp.where(lane < i, gate, -jnp.inf)
    cnt = jnp.zeros(gate.shape, jnp.int32)
    for c in range(nb - 1):
        gc = gate[:, c:c + 1]
        beats = (gc > gate) | ((gc == gate) & (c < lane))
        cnt = cnt + jnp.where(beats, 1, 0)
    selbias = jnp.where((cnt < MOBA_TOPK) & (lane < i), 0.0, NEG)

    qs = (q * (HEAD_DIM ** -0.5)).astype(BF16)
    s = _dot_nt(qs, kr_ref[pl.ds(row0, blk), :])
    qpos = lax.broadcasted_iota(jnp.int32, s.shape, 0)
    kpos = lax.broadcasted_iota(jnp.int32, s.shape, 1)
    s = jnp.where(kpos <= qpos, s, NEG)
    m0 = jnp.max(s, axis=1, keepdims=True)
    p = jnp.exp(s - m0)
    l0 = jnp.sum(p, axis=1, keepdims=True)
    acc0 = _dot(p.astype(BF16), v_ref[pl.ds(row0, blk), :].astype(BF16))

    def body(j, carry):
        m, l, acc = carry
        c0 = pl.multiple_of(j * blk, blk)
        bias = jnp.max(jnp.where(lane == j, selbias, NEG), axis=1, keepdims=True)
        sj = _dot_nt(qs, kr_ref[pl.ds(c0, blk), :]) + bias
        mn = jnp.maximum(m, jnp.max(sj, axis=1, keepdims=True))
        alpha = jnp.exp(m - mn)
        pj = jnp.exp(sj - mn)
        l = alpha * l + jnp.sum(pj, axis=1, keepdims=True)
        acc = alpha * acc + _dot(pj.astype(BF16), v_ref[pl.ds(c0, blk), :].astype(BF16))
        return mn, l, acc

    _, l, acc = lax.fori_loop(0, i, body, (m0, l0, acc0))
    o_ref[...] = (acc / l).astype(o_ref.dtype)


def _moba(proj, cosf, sinf, bsz, seqlen):
    nb = seqlen // MOBA_BLOCK
    hb = HEAD_DIM // LANES
    assert hb == 1
    return pl.pallas_call(
        functools.partial(_moba_kernel, nb=nb),
        grid=(bsz, ATTN_HEADS, nb),
        in_specs=[pl.BlockSpec((MOBA_BLOCK, HEAD_DIM), lambda b, h, i: (b * nb + i, COL_Q // HEAD_DIM + h)),
                  pl.BlockSpec((seqlen, HEAD_DIM), lambda b, h, i: (b, COL_K // HEAD_DIM + h)),
                  pl.BlockSpec((seqlen, HEAD_DIM), lambda b, h, i: (b, COL_V // HEAD_DIM + h)),
                  pl.BlockSpec((seqlen, HEAD_DIM), lambda b, h, i: (0, 0)),
                  pl.BlockSpec((seqlen, HEAD_DIM), lambda b, h, i: (0, 0))],
        out_specs=pl.BlockSpec((MOBA_BLOCK, HEAD_DIM), lambda b, h, i: (b * nb + i, h)),
        out_shape=jax.ShapeDtypeStruct((bsz * seqlen, ATTN_WIDTH), BF16),
        scratch_shapes=[pltpu.VMEM((seqlen, HEAD_DIM), BF16),
                        pltpu.VMEM((LANES, HEAD_DIM), F32)],
        compiler_params=_params("parallel", "parallel", "arbitrary"),
        name="moba",
    )(proj, proj, proj, cosf, sinf)


def _ssd_kernel(xbc_ref, z_ref, dt_ref, cw_ref, cb_ref, dtb_ref, alog_ref, dch_ref, nw_ref, ex_ref,
                o_ref, xpad_ref, st_ref, y_ref):
    c = pl.program_id(1)
    cl = SSM_CHUNK
    hd = SSM_HEAD_DIM
    gw = SSM_WIDTH // SSM_GROUPS
    hg = SSM_HEADS // SSM_GROUPS

    @pl.when(c == 0)
    def _():
        xpad_ref[0:8, :] = jnp.zeros((8, SSM_CONV_DIM), F32)
        st_ref[...] = jnp.zeros_like(st_ref)

    @pl.when(c > 0)
    def _():
        xpad_ref[0:8, :] = xpad_ref[cl:cl + 8, :]

    xpad_ref[8:8 + cl, :] = xbc_ref[...]
    conv = jnp.zeros((cl, SSM_CONV_DIM), F32) + cb_ref[...]
    for k in range(SSM_CONV):
        off = 8 - (SSM_CONV - 1) + k
        conv = conv + xpad_ref[off:off + cl, :] * cw_ref[k:k + 1, :]
    act = _silu(conv)
    xs = act[:, :SSM_WIDTH]

    dt = jax.nn.softplus(dt_ref[...] + dtb_ref[...])
    la = dt * (-jnp.exp(alog_ref[...]))
    ti = lax.broadcasted_iota(jnp.int32, (cl, cl), 0)
    si = lax.broadcasted_iota(jnp.int32, (cl, cl), 1)
    causal = si <= ti
    cum = _dot_hi(jnp.where(causal, 1.0, 0.0), la)
    cum_t = cum.T
    ex = ex_ref[...]
    dt_c = _dot_hi(dt, ex)
    cum_c = _dot_hi(cum, ex)
    cum_last = cum_c[cl - 1:cl, :]
    xr = xs * dt_c
    xw = xr * jnp.exp(cum_last - cum_c)
    e_cum = jnp.exp(cum_c)
    e_last = jnp.exp(cum_last)

    for g in range(SSM_GROUPS):
        bm = act[:, SSM_WIDTH + g * SSM_STATE:SSM_WIDTH + (g + 1) * SSM_STATE]
        cm = act[:, SSM_WIDTH + (SSM_GROUPS + g) * SSM_STATE:SSM_WIDTH + (SSM_GROUPS + g + 1) * SSM_STATE]
        bm_t = bm.T
        cb = _dot(cm, bm_t)
        gs = slice(g * gw, (g + 1) * gw)
        st = st_ref[g]
        y_ref[:, gs] = _dot(cm, st) * e_cum[:, gs]
        st_ref[g] = st * e_last[:, gs] + _dot(bm_t, xw[:, gs])
        for hh in range(hg):
            h = g * hg + hh
            seg = cum[:, h:h + 1] - cum_t[h:h + 1, :]
            decay = jnp.exp(jnp.where(causal, seg, -jnp.inf))
            hs = slice(h * hd, (h + 1) * hd)
            y_ref[:, hs] = y_ref[:, hs] + _dot(cb * decay, xr[:, hs])

    y = (y_ref[...] + dch_ref[...] * xs) * _silu(z_ref[...])
    for g in range(SSM_GROUPS):
        gs = slice(g * gw, (g + 1) * gw)
        yg = y[:, gs]
        yg = yg * lax.rsqrt(jnp.mean(yg * yg, axis=-1, keepdims=True) + EPS)
        o_ref[:, gs] = (yg * nw_ref[:, gs]).astype(o_ref.dtype)


def _ssd(proj, conv_w, conv_b, dt_bias, a_log, d_skip, norm_w, bsz, seqlen):
    nc = seqlen // SSM_CHUNK
    pad = LANES - SSM_HEADS
    dtb = jnp.pad(dt_bias, (0, pad)).reshape(1, LANES)
    alog = jnp.pad(a_log, (0, pad)).reshape(1, LANES)
    dch = jnp.repeat(d_skip, SSM_HEAD_DIM).reshape(1, SSM_WIDTH)
    expand = (jnp.arange(LANES)[:, None] == (jnp.arange(SSM_WIDTH)[None, :] // SSM_HEAD_DIM)).astype(F32)
    const = lambda shape: pl.BlockSpec(shape, lambda b, c: (0, 0))
    return pl.pallas_call(
        _ssd_kernel,
        grid=(bsz, nc),
        in_specs=[pl.BlockSpec((SSM_CHUNK, SSM_CONV_DIM), lambda b, c: (b * nc + c, COL_XBC // SSM_CONV_DIM)),
                  pl.BlockSpec((SSM_CHUNK, SSM_WIDTH), lambda b, c: (b * nc + c, COL_Z // SSM_WIDTH)),
                  pl.BlockSpec((SSM_CHUNK, LANES), lambda b, c: (b * nc + c, COL_DT // LANES)),
                  const((SSM_CONV, SSM_CONV_DIM)), const((1, SSM_CONV_DIM)),
                  const((1, LANES)), const((1, LANES)), const((1, SSM_WIDTH)), const((1, SSM_WIDTH)),
                  const((LANES, SSM_WIDTH))],
        out_specs=pl.BlockSpec((SSM_CHUNK, SSM_WIDTH), lambda b, c: (b * nc + c, 0)),
        out_shape=jax.ShapeDtypeStruct((bsz * seqlen, SSM_WIDTH), BF16),
        scratch_shapes=[pltpu.VMEM((SSM_CHUNK + 8, SSM_CONV_DIM), F32),
                        pltpu.VMEM((SSM_GROUPS, SSM_STATE, SSM_WIDTH // SSM_GROUPS), F32),
                        pltpu.VMEM((SSM_CHUNK, SSM_WIDTH), F32)],
        compiler_params=_params("parallel", "arbitrary"),
        name="ssd",
    )(proj, proj, proj, conv_w, conv_b.reshape(1, -1), dtb, alog, dch, norm_w.reshape(1, -1), expand)


CF_TILE = 256
CF_HALO = 32
CF_ROWS = 64


def _conformer_kernel(u_ref, cw_ref, cb_ref, lw_ref, lb_ref, o_ref, buf_ref, cv_ref):
    i = pl.program_id(1)
    ts = CF_TILE

    @pl.when(i == 0)
    def _():
        buf_ref[0:CF_HALO, :] = jnp.zeros((CF_HALO, CONV_WIDTH), F32)

    @pl.when(i > 0)
    def _():
        buf_ref[0:CF_HALO, :] = buf_ref[ts:ts + CF_HALO, :]

    buf_ref[CF_HALO:CF_HALO + ts, :] = u_ref[:, :CONV_WIDTH] * jax.nn.sigmoid(u_ref[:, CONV_WIDTH:])
    base = CF_HALO - (CONV_SIZE - 1)
    for r in range(ts // CF_ROWS):
        for cc in range(CONV_WIDTH // LANES):
            cs = slice(cc * LANES, (cc + 1) * LANES)
            acc = jnp.zeros((CF_ROWS, LANES), F32) + cb_ref[:, cs]
            for k in range(CONV_SIZE):
                off = r * CF_ROWS + base + k
                acc = acc + buf_ref[off:off + CF_ROWS, cs] * cw_ref[k:k + 1, cs]
            cv_ref[r * CF_ROWS:(r + 1) * CF_ROWS, cs] = acc
    hf = cv_ref[...]
    mu = jnp.mean(hf, axis=-1, keepdims=True)
    d = hf - mu
    var = jnp.mean(d * d, axis=-1, keepdims=True)
    o_ref[...] = _silu(d * lax.rsqrt(var + EPS) * lw_ref[...] + lb_ref[...]).astype(o_ref.dtype)


def _conformer(proj, conv_w, conv_b, ln_w, ln_b, bsz, seqlen):
    nt = seqlen // CF_TILE
    const = lambda shape: pl.BlockSpec(shape, lambda b, i: (0, 0))
    return pl.pallas_call(
        _conformer_kernel,
        grid=(bsz, nt),
        in_specs=[pl.BlockSpec((CF_TILE, 2 * CONV_WIDTH), lambda b, i: (b * nt + i, COL_GLU // (2 * CONV_WIDTH))),
                  const((CONV_SIZE, CONV_WIDTH)), const((1, CONV_WIDTH)),
                  const((1, CONV_WIDTH)), const((1, CONV_WIDTH))],
        out_specs=pl.BlockSpec((CF_TILE, CONV_WIDTH), lambda b, i: (b * nt + i, 0)),
        out_shape=jax.ShapeDtypeStruct((bsz * seqlen, CONV_WIDTH), BF16),
        scratch_shapes=[pltpu.VMEM((CF_HALO + CF_TILE, CONV_WIDTH), F32),
                        pltpu.VMEM((CF_TILE, CONV_WIDTH), F32)],
        compiler_params=_params("parallel", "arbitrary"),
        name="conformer",
    )(proj, conv_w, conv_b.reshape(1, -1), ln_w.reshape(1, -1), ln_b.reshape(1, -1))


def _outproj_kernel(a_ref, s_ref, c_ref, wa_ref, ws_ref, wc_ref, h_ref, o_ref):
    o_ref[...] = (h_ref[...] + _dot(a_ref[...], wa_ref[...]) + _dot(s_ref[...], ws_ref[...])
                  + _dot(c_ref[...], wc_ref[...]))


def _outproj(attn, ssm, conv, w_out, h, tm=512):
    t = h.shape[0]
    row = lambda w: pl.BlockSpec((tm, w), lambda i: (i, 0))
    return pl.pallas_call(
        _outproj_kernel,
        grid=(t // tm,),
        in_specs=[row(ATTN_WIDTH), row(SSM_WIDTH), row(CONV_WIDTH),
                  pl.BlockSpec((ATTN_WIDTH, D_MODEL), lambda i: (0, 0)),
                  pl.BlockSpec((SSM_WIDTH, D_MODEL), lambda i: (ATTN_WIDTH // SSM_WIDTH, 0)),
                  pl.BlockSpec((CONV_WIDTH, D_MODEL), lambda i: ((ATTN_WIDTH + SSM_WIDTH) // CONV_WIDTH, 0)),
                  row(D_MODEL)],
        out_specs=row(D_MODEL),
        out_shape=jax.ShapeDtypeStruct((t, D_MODEL), F32),
        compiler_params=_params("parallel"),
        name="outproj",
    )(attn, ssm, conv, w_out, w_out, w_out, h)


def _ffn_kernel(h_ref, nw_ref, wg_ref, wu_ref, wd_ref, o_ref, hn_ref):
    @pl.when(pl.program_id(1) == 0)
    def _():
        h = h_ref[...]
        hn_ref[...] = _rms(h, nw_ref[...]).astype(BF16)
        o_ref[...] = h

    hn = hn_ref[...]
    a = _silu(_dot(hn, wg_ref[...])) * _dot(hn, wu_ref[...])
    o_ref[...] += _dot(a.astype(BF16), wd_ref[...])


def _ffn(h, nw, wg, wu, wd, tm=1024, tf=512):
    t = h.shape[0]
    return pl.pallas_call(
        _ffn_kernel,
        grid=(t // tm, D_FF // tf),
        in_specs=[pl.BlockSpec((tm, D_MODEL), lambda i, f: (i, 0)),
                  pl.BlockSpec((1, D_MODEL), lambda i, f: (0, 0)),
                  pl.BlockSpec((D_MODEL, tf), lambda i, f: (0, f)),
                  pl.BlockSpec((D_MODEL, tf), lambda i, f: (0, f)),
                  pl.BlockSpec((tf, D_MODEL), lambda i, f: (f, 0))],
        out_specs=pl.BlockSpec((tm, D_MODEL), lambda i, f: (i, 0)),
        out_shape=jax.ShapeDtypeStruct((t, D_MODEL), F32),
        scratch_shapes=[pltpu.VMEM((tm, D_MODEL), BF16)],
        compiler_params=_params("parallel", "arbitrary"),
        name="ffn",
    )(h, nw, wg, wu, wd)


def _router_kernel(h_ref, nw_ref, wr_ref, o_ref):
    hn = _rms(h_ref[...], nw_ref[...])
    logits = _dot_hi(hn, wr_ref[...])
    lane = lax.broadcasted_iota(jnp.int32, logits.shape, 1)
    logits = jnp.where(lane < N_EXPERTS, logits, -jnp.inf)
    m1 = jnp.max(logits, axis=1, keepdims=True)
    i1 = jnp.min(jnp.where(logits == m1, lane, LANES), axis=1, keepdims=True)
    rest = jnp.where(lane == i1, -jnp.inf, logits)
    m2 = jnp.max(rest, axis=1, keepdims=True)
    i2 = jnp.min(jnp.where(rest == m2, lane, LANES), axis=1, keepdims=True)
    e2 = jnp.exp(m2 - m1)
    den = 1.0 + e2
    o_ref[...] = jnp.where(lane == i1, 1.0 / den, 0.0) + jnp.where(lane == i2, e2 / den, 0.0)


def _router(h, nw, wr, tm=512):
    t = h.shape[0]
    wr_p = jnp.pad(wr, ((0, 0), (0, LANES - N_EXPERTS)))
    return pl.pallas_call(
        _router_kernel,
        grid=(t // tm,),
        in_specs=[pl.BlockSpec((tm, D_MODEL), lambda i: (i, 0)),
                  pl.BlockSpec((1, D_MODEL), lambda i: (0, 0)),
                  pl.BlockSpec((D_MODEL, LANES), lambda i: (0, 0))],
        out_specs=pl.BlockSpec((tm, LANES), lambda i: (i, 0)),
        out_shape=jax.ShapeDtypeStruct((t, LANES), F32),
        compiler_params=_params("parallel"),
        name="router",
    )(h, nw, wr_p)


def _moe_kernel(h_ref, nw_ref, comb_ref, wg_ref, wu_ref, wd_ref, fw_ref, o_ref, hn_ref):
    e = pl.program_id(1)
    f = pl.program_id(2)

    @pl.when((e == 0) & (f == 0))
    def _():
        h = h_ref[...]
        hn_ref[...] = _rms(h, nw_ref[...]).astype(BF16)
        o_ref[...] = h

    comb = comb_ref[...]
    lane = lax.broadcasted_iota(jnp.int32, comb.shape, 1)
    ce = jnp.sum(jnp.where(lane == e, comb, 0.0), axis=1, keepdims=True)
    hn = hn_ref[...]
    a = _silu(_dot(hn, wg_ref[...])) * _dot(hn, wu_ref[...]) * ce
    o_ref[...] += _dot(a.astype(BF16), wd_ref[...])

    @pl.when((e == pl.num_programs(1) - 1) & (f == pl.num_programs(2) - 1))
    def _():
        o_ref[...] = _rms(o_ref[...], fw_ref[...])


def _moe(h, nw, comb, wg, wu, wd, fw, tm=1024, tf=512):
    t = h.shape[0]
    return pl.pallas_call(
        _moe_kernel,
        grid=(t // tm, N_EXPERTS, D_FF // tf),
        in_specs=[pl.BlockSpec((tm, D_MODEL), lambda i, e, f: (i, 0)),
                  pl.BlockSpec((1, D_MODEL), lambda i, e, f: (0, 0)),
                  pl.BlockSpec((tm, LANES), lambda i, e, f: (i, 0)),
                  pl.BlockSpec((None, D_MODEL, tf), lambda i, e, f: (e, 0, f)),
                  pl.BlockSpec((None, D_MODEL, tf), lambda i, e, f: (e, 0, f)),
                  pl.BlockSpec((None, tf, D_MODEL), lambda i, e, f: (e, f, 0)),
                  pl.BlockSpec((1, D_MODEL), lambda i, e, f: (0, 0))],
        out_specs=pl.BlockSpec((tm, D_MODEL), lambda i, e, f: (i, 0)),
        out_shape=jax.ShapeDtypeStruct((t, D_MODEL), F32),
        scratch_shapes=[pltpu.VMEM((tm, D_MODEL), BF16)],
        compiler_params=_params("parallel", "arbitrary", "arbitrary"),
        name="moe",
    )(h, nw, comb, wg, wu, wd, fw)


def _rope_tables(seqlen):
    pos = jnp.arange(seqlen, dtype=F32)
    inv_freq = 1.0 / (ROPE_THETA ** (jnp.arange(0, HEAD_DIM, 2, dtype=F32) / HEAD_DIM))
    ang = pos[:, None] * inv_freq[None, :]
    cos, sin = jnp.cos(ang), jnp.sin(ang)
    return jnp.concatenate([cos, cos], axis=-1), jnp.concatenate([-sin, sin], axis=-1)


def _arrange_w_in(w):
    q_k_v = w[:, :3 * ATTN_WIDTH]
    z = w[:, 3072:3584]
    xbc = w[:, 3584:4608]
    dt = jnp.pad(w[:, 4608:4616], ((0, 0), (0, LANES - SSM_HEADS)))
    glu = w[:, 4616:5640]
    return jnp.concatenate([q_k_v, xbc, glu, z, dt], axis=1).astype(BF16)


def kernel(x, norm_mix, w_in, ssm_conv_w, ssm_conv_b, ssm_dt_bias, ssm_a_log, ssm_d, ssm_norm_w, cf_conv_w, cf_conv_b, cf_ln_w, cf_ln_b, w_out, norm_ffn, ffn_w_gate, ffn_w_up, ffn_w_down, moe_router, moe_w_gate, moe_w_up, moe_w_down, norm_final):
    bsz, seqlen, _ = x.shape
    depth = w_in.shape[0]
    assert depth == 2 and ffn_w_gate.shape[0] == 1 and moe_router.shape[0] == 1
    cosf, sinf = _rope_tables(seqlen)
    h = x.reshape(bsz * seqlen, D_MODEL)
    for layer in range(depth):
        proj = _inproj(h, norm_mix[layer].reshape(1, -1), _arrange_w_in(w_in[layer]))
        attn = _moba(proj, cosf, sinf, bsz, seqlen)
        ssm = _ssd(proj, ssm_conv_w[layer], ssm_conv_b[layer], ssm_dt_bias[layer], ssm_a_log[layer],
                   ssm_d[layer], ssm_norm_w[layer], bsz, seqlen)
        conv = _conformer(proj, cf_conv_w[layer], cf_conv_b[layer], cf_ln_w[layer], cf_ln_b[layer], bsz, seqlen)
        h = _outproj(attn, ssm, conv, w_out[layer].astype(BF16), h)
        nw = norm_ffn[layer].reshape(1, -1)
        if layer % 2 == 0:
            i = layer // 2
            h = _ffn(h, nw, ffn_w_gate[i].astype(BF16), ffn_w_up[i].astype(BF16), ffn_w_down[i].astype(BF16))
        else:
            i = layer // 2
            comb = _router(h, nw, moe_router[i])
            h = _moe(h, nw, comb, moe_w_gate[i].astype(BF16), moe_w_up[i].astype(BF16),
                     moe_w_down[i].astype(BF16), norm_final.reshape(1, -1))
    return h.reshape(bsz, seqlen, D_MODEL)
```

```python
import functools
import math

import jax
import jax.numpy as jnp
from jax import lax
from jax.experimental import pallas as pl
from jax.experimental.pallas import tpu as pltpu

F32 = jnp.float32
BF16 = jnp.bfloat16

D_MODEL = 2048
HEAD_DIM = 128
ATTN_WIDTH = 1024
ATTN_HEADS = 8
MOBA_BLOCK = 256
MOBA_TOPK = 3
ROPE_THETA = 10000.0
SSM_WIDTH = 512
SSM_HEAD_DIM = 64
SSM_HEADS = 8
SSM_GROUPS = 2
SSM_STATE = 128
SSM_CONV = 4
SSM_CHUNK = 128
SSM_CONV_DIM = 1024
CONV_WIDTH = 512
CONV_SIZE = 31
D_FF = 7168
N_EXPERTS = 8
EPS = 1e-5

LANES = 128
NEG = -1e30

COL_Q, COL_K, COL_V = 0, 1024, 2048
COL_XBC, COL_GLU, COL_Z, COL_DT = 3072, 4096, 5120, 5632
IN_PAD = 5760

VMEM_LIMIT = 56 * 1024 * 1024


def _params(*sem):
    return pltpu.CompilerParams(dimension_semantics=sem, vmem_limit_bytes=VMEM_LIMIT)


def _dot(a, b):
    return jnp.dot(a, b, preferred_element_type=F32)


def _dot_nt(a, b):
    return lax.dot_general(a, b, (((1,), (1,)), ((), ())), preferred_element_type=F32)


def _dot_hi(a, b):
    return jnp.dot(a, b, preferred_element_type=F32, precision=lax.Precision.HIGHEST)


def _dot_nt_hi(a, b):
    return lax.dot_general(a, b, (((1,), (1,)), ((), ())), preferred_element_type=F32,
                           precision=lax.Precision.HIGHEST)


def _rms(x, w):
    return x * lax.rsqrt(jnp.mean(x * x, axis=-1, keepdims=True) + EPS) * w


def _silu(x):
    return x * jax.nn.sigmoid(x)


def _inproj_kernel(x_ref, nw_ref, w_ref, o_ref, xn_ref):
    @pl.when(pl.program_id(1) == 0)
    def _():
        xn_ref[...] = _rms(x_ref[...], nw_ref[...]).astype(BF16)

    o_ref[...] = _dot(xn_ref[...], w_ref[...])


def _inproj(h, nw, w, tm=512, tn=1152):
    t = h.shape[0]
    return pl.pallas_call(
        _inproj_kernel,
        grid=(t // tm, IN_PAD // tn),
        in_specs=[pl.BlockSpec((tm, D_MODEL), lambda i, j: (i, 0)),
                  pl.BlockSpec((1, D_MODEL), lambda i, j: (0, 0)),
                  pl.BlockSpec((D_MODEL, tn), lambda i, j: (0, j))],
        out_specs=pl.BlockSpec((tm, tn), lambda i, j: (i, j)),
        out_shape=jax.ShapeDtypeStruct((t, IN_PAD), F32),
        scratch_shapes=[pltpu.VMEM((tm, D_MODEL), BF16)],
        compiler_params=_params("parallel", "arbitrary"),
        name="inproj",
    )(h, nw, w)


def _rope(x, cosf, sinf):
    return x * cosf + pltpu.roll(x, HEAD_DIM // 2, 1) * sinf


def _moba_kernel(q_ref, k_ref, v_ref, cos_ref, sin_ref, o_ref, kr_ref, km_ref, *, nb):
    i = pl.program_id(2)
    blk = MOBA_BLOCK

    @pl.when(i == 0)
    def _():
        km_ref[...] = jnp.zeros_like(km_ref)
        for j in range(nb):
            sl = slice(j * blk, (j + 1) * blk)
            kr = _rope(k_ref[sl, :], cos_ref[sl, :], sin_ref[sl, :])
            kr_ref[sl, :] = kr.astype(BF16)
            km_ref[j:j + 1, :] = jnp.mean(kr, axis=0, keepdims=True)

    row0 = pl.multiple_of(i * blk, blk)
    q = _rope(q_ref[...], cos_ref[pl.ds(row0, blk), :], sin_ref[pl.ds(row0, blk), :])
    gate = _dot_nt_hi(q, km_ref[...])
    lane = lax.broadcasted_iota(jnp.int32, gate.shape, 1)
    gate = jnp.where(lane < i, gate, -jnp.inf)
    cnt = jnp.zeros(gate.shape, jnp.int32)
    for c in range(nb - 1):
        gc = gate[:, c:c + 1]
        beats = (gc > gate) | ((gc == gate) & (c < lane))
        cnt = cnt + jnp.where(beats, 1, 0)
    selbias = jnp.where((cnt < MOBA_TOPK) & (lane < i), 0.0, NEG)

    qs = (q * (HEAD_DIM ** -0.5)).astype(BF16)
    s = _dot_nt(qs, kr_ref[pl.ds(row0, blk), :])
    qpos = lax.broadcasted_iota(jnp.int32, s.shape, 0)
    kpos = lax.broadcasted_iota(jnp.int32, s.shape, 1)
    s = jnp.where(kpos <= qpos, s, NEG)
    m0 = jnp.max(s, axis=1, keepdims=True)
    p = jnp.exp(s - m0)
    l0 = jnp.sum(p, axis=1, keepdims=True)
    acc0 = _dot(p.astype(BF16), v_ref[pl.ds(row0, blk), :].astype(BF16))

    def body(j, carry):
        m, l, acc = carry
        c0 = pl.multiple_of(j * blk, blk)
        bias = jnp.max(jnp.where(lane == j, selbias, NEG), axis=1, keepdims=True)
        sj = _dot_nt(qs, kr_ref[pl.ds(c0, blk), :]) + bias
        mn = jnp.maximum(m, jnp.max(sj, axis=1, keepdims=True))
        alpha = jnp.exp(m - mn)
        pj = jnp.exp(sj - mn)
        l = alpha * l + jnp.sum(pj, axis=1, keepdims=True)
        acc = alpha * acc + _dot(pj.astype(BF16), v_ref[pl.ds(c0, blk), :].astype(BF16))
        return mn, l, acc

    _, l, acc = lax.fori_loop(0, i, body, (m0, l0, acc0))
    o_ref[...] = (acc / l).astype(o_ref.dtype)


def _moba(proj, cosf, sinf, bsz, seqlen):
    nb = seqlen // MOBA_BLOCK
    hb = HEAD_DIM // LANES
    assert hb == 1
    return pl.pallas_call(
        functools.partial(_moba_kernel, nb=nb),
        grid=(bsz, ATTN_HEADS, nb),
        in_specs=[pl.BlockSpec((MOBA_BLOCK, HEAD_DIM), lambda b, h, i: (b * nb + i, COL_Q // HEAD_DIM + h)),
                  pl.BlockSpec((seqlen, HEAD_DIM), lambda b, h, i: (b, COL_K // HEAD_DIM + h)),
                  pl.BlockSpec((seqlen, HEAD_DIM), lambda b, h, i: (b, COL_V // HEAD_DIM + h)),
                  pl.BlockSpec((seqlen, HEAD_DIM), lambda b, h, i: (0, 0)),
                  pl.BlockSpec((seqlen, HEAD_DIM), lambda b, h, i: (0, 0))],
        out_specs=pl.BlockSpec((MOBA_BLOCK, HEAD_DIM), lambda b, h, i: (b * nb + i, h)),
        out_shape=jax.ShapeDtypeStruct((bsz * seqlen, ATTN_WIDTH), BF16),
        scratch_shapes=[pltpu.VMEM((seqlen, HEAD_DIM), BF16),
                        pltpu.VMEM((LANES, HEAD_DIM), F32)],
        compiler_params=_params("parallel", "parallel", "arbitrary"),
        name="moba",
    )(proj, proj, proj, cosf, sinf)


def _ssd_kernel(xbc_ref, z_ref, dt_ref, cw_ref, cb_ref, dtb_ref, alog_ref, dch_ref, nw_ref, ex_ref,
                o_ref, xpad_ref, st_ref, y_ref):
    c = pl.program_id(1)
    cl = SSM_CHUNK
    hd = SSM_HEAD_DIM
    gw = SSM_WIDTH // SSM_GROUPS
    hg = SSM_HEADS // SSM_GROUPS

    @pl.when(c == 0)
    def _():
        xpad_ref[0:8, :] = jnp.zeros((8, SSM_CONV_DIM), F32)
        st_ref[...] = jnp.zeros_like(st_ref)

    @pl.when(c > 0)
    def _():
        xpad_ref[0:8, :] = xpad_ref[cl:cl + 8, :]

    xpad_ref[8:8 + cl, :] = xbc_ref[...]
    conv = jnp.zeros((cl, SSM_CONV_DIM), F32) + cb_ref[...]
    for k in range(SSM_CONV):
        off = 8 - (SSM_CONV - 1) + k
        conv = conv + xpad_ref[off:off + cl, :] * cw_ref[k:k + 1, :]
    act = _silu(conv)
    xs = act[:, :SSM_WIDTH]

    dt = jax.nn.softplus(dt_ref[...] + dtb_ref[...])
    la = dt * (-jnp.exp(alog_ref[...]))
    ti = lax.broadcasted_iota(jnp.int32, (cl, cl), 0)
    si = lax.broadcasted_iota(jnp.int32, (cl, cl), 1)
    causal = si <= ti
    cum = _dot_hi(jnp.where(causal, 1.0, 0.0), la)
    cum_t = cum.T
    ex = ex_ref[...]
    dt_c = _dot_hi(dt, ex)
    cum_c = _dot_hi(cum, ex)
    cum_last = cum_c[cl - 1:cl, :]
    xr = xs * dt_c
    xw = xr * jnp.exp(cum_last - cum_c)
    e_cum = jnp.exp(cum_c)
    e_last = jnp.exp(cum_last)

    for g in range(SSM_GROUPS):
        bm = act[:, SSM_WIDTH + g * SSM_STATE:SSM_WIDTH + (g + 1) * SSM_STATE]
        cm = act[:, SSM_WIDTH + (SSM_GROUPS + g) * SSM_STATE:SSM_WIDTH + (SSM_GROUPS + g + 1) * SSM_STATE]
        bm_t = bm.T
        cb = _dot(cm, bm_t)
        gs = slice(g * gw, (g + 1) * gw)
        st = st_ref[g]
        y_ref[:, gs] = _dot(cm, st) * e_cum[:, gs]
        st_ref[g] = st * e_last[:, gs] + _dot(bm_t, xw[:, gs])
        for hh in range(hg):
            h = g * hg + hh
            seg = cum[:, h:h + 1] - cum_t[h:h + 1, :]
            decay = jnp.exp(jnp.where(causal, seg, -jnp.inf))
            hs = slice(h * hd, (h + 1) * hd)
            y_ref[:, hs] = y_ref[:, hs] + _dot(cb * decay, xr[:, hs])

    y = (y_ref[...] + dch_ref[...] * xs) * _silu(z_ref[...])
    for g in range(SSM_GROUPS):
        gs = slice(g * gw, (g + 1) * gw)
        yg = y[:, gs]
        yg = yg * lax.rsqrt(jnp.mean(yg * yg, axis=-1, keepdims=True) + EPS)
        o_ref[:, gs] = (yg * nw_ref[:, gs]).astype(o_ref.dtype)


def _ssd(proj, conv_w, conv_b, dt_bias, a_log, d_skip, norm_w, bsz, seqlen):
    nc = seqlen // SSM_CHUNK
    pad = LANES - SSM_HEADS
    dtb = jnp.pad(dt_bias, (0, pad)).reshape(1, LANES)
    alog = jnp.pad(a_log, (0, pad)).reshape(1, LANES)
    dch = jnp.repeat(d_skip, SSM_HEAD_DIM).reshape(1, SSM_WIDTH)
    expand = (jnp.arange(LANES)[:, None] == (jnp.arange(SSM_WIDTH)[None, :] // SSM_HEAD_DIM)).astype(F32)
    const = lambda shape: pl.BlockSpec(shape, lambda b, c: (0, 0))
    return pl.pallas_call(
        _ssd_kernel,
        grid=(bsz, nc),
        in_specs=[pl.BlockSpec((SSM_CHUNK, SSM_CONV_DIM), lambda b, c: (b * nc + c, COL_XBC // SSM_CONV_DIM)),
                  pl.BlockSpec((SSM_CHUNK, SSM_WIDTH), lambda b, c: (b * nc + c, COL_Z // SSM_WIDTH)),
                  pl.BlockSpec((SSM_CHUNK, LANES), lambda b, c: (b * nc + c, COL_DT // LANES)),
                  const((SSM_CONV, SSM_CONV_DIM)), const((1, SSM_CONV_DIM)),
                  const((1, LANES)), const((1, LANES)), const((1, SSM_WIDTH)), const((1, SSM_WIDTH)),
                  const((LANES, SSM_WIDTH))],
        out_specs=pl.BlockSpec((SSM_CHUNK, SSM_WIDTH), lambda b, c: (b * nc + c, 0)),
        out_shape=jax.ShapeDtypeStruct((bsz * seqlen, SSM_WIDTH), BF16),
        scratch_shapes=[pltpu.VMEM((SSM_CHUNK + 8, SSM_CONV_DIM), F32),
                        pltpu.VMEM((SSM_GROUPS, SSM_STATE, SSM_WIDTH // SSM_GROUPS), F32),
                        pltpu.VMEM((SSM_CHUNK, SSM_WIDTH), F32)],
        compiler_params=_params("parallel", "arbitrary"),
        name="ssd",
    )(proj, proj, proj, conv_w, conv_b.reshape(1, -1), dtb, alog, dch, norm_w.reshape(1, -1), expand)


CF_TILE = 256
CF_HALO = 32
CF_ROWS = 64


def _conformer_kernel(u_ref, cw_ref, cb_ref, lw_ref, lb_ref, o_ref, buf_ref, cv_ref):
    i = pl.program_id(1)
    ts = CF_TILE

    @pl.when(i == 0)
    def _():
        buf_ref[0:CF_HALO, :] = jnp.zeros((CF_HALO, CONV_WIDTH), F32)

    @pl.when(i > 0)
    def _():
        buf_ref[0:CF_HALO, :] = buf_ref[ts:ts + CF_HALO, :]

    buf_ref[CF_HALO:CF_HALO + ts, :] = u_ref[:, :CONV_WIDTH] * jax.nn.sigmoid(u_ref[:, CONV_WIDTH:])
    base = CF_HALO - (CONV_SIZE - 1)
    for r in range(ts // CF_ROWS):
        for cc in range(CONV_WIDTH // LANES):
            cs = slice(cc * LANES, (cc + 1) * LANES)
            acc = jnp.zeros((CF_ROWS, LANES), F32) + cb_ref[:, cs]
            for k in range(CONV_SIZE):
                off = r * CF_ROWS + base + k
                acc = acc + buf_ref[off:off + CF_ROWS, cs] * cw_ref[k:k + 1, cs]
            cv_ref[r * CF_ROWS:(r + 1) * CF_ROWS, cs] = acc
    hf = cv_ref[...]
    mu = jnp.mean(hf, axis=-1, keepdims=True)
    d = hf - mu
    var = jnp.mean(d * d, axis=-1, keepdims=True)
    o_ref[...] = _silu(d * lax.rsqrt(var + EPS) * lw_ref[...] + lb_ref[...]).astype(o_ref.dtype)


def _conformer(proj, conv_w, conv_b, ln_w, ln_b, bsz, seqlen):
    nt = seqlen // CF_TILE
    const = lambda shape: pl.BlockSpec(shape, lambda b, i: (0, 0))
    return pl.pallas_call(
        _conformer_kernel,
        grid=(bsz, nt),
        in_specs=[pl.BlockSpec((CF_TILE, 2 * CONV_WIDTH), lambda b, i: (b * nt + i, COL_GLU // (2 * CONV_WIDTH))),
                  const((CONV_SIZE, CONV_WIDTH)), const((1, CONV_WIDTH)),
                  const((1, CONV_WIDTH)), const((1, CONV_WIDTH))],
        out_specs=pl.BlockSpec((CF_TILE, CONV_WIDTH), lambda b, i: (b * nt + i, 0)),
        out_shape=jax.ShapeDtypeStruct((bsz * seqlen, CONV_WIDTH), BF16),
        scratch_shapes=[pltpu.VMEM((CF_HALO + CF_TILE, CONV_WIDTH), F32),
                        pltpu.VMEM((CF_TILE, CONV_WIDTH), F32)],
        compiler_params=_params("parallel", "arbitrary"),
        name="conformer",
    )(proj, conv_w, conv_b.reshape(1, -1), ln_w.reshape(1, -1), ln_b.reshape(1, -1))


def _outproj_kernel(a_ref, s_ref, c_ref, wa_ref, ws_ref, wc_ref, h_ref, o_ref):
    o_ref[...] = (h_ref[...] + _dot(a_ref[...], wa_ref[...]) + _dot(s_ref[...], ws_ref[...])
                  + _dot(c_ref[...], wc_ref[...]))


def _outproj(attn, ssm, conv, w_out, h, tm=512):
    t = h.shape[0]
    row = lambda w: pl.BlockSpec((tm, w), lambda i: (i, 0))
    return pl.pallas_call(
        _outproj_kernel,
        grid=(t // tm,),
        in_specs=[row(ATTN_WIDTH), row(SSM_WIDTH), row(CONV_WIDTH),
                  pl.BlockSpec((ATTN_WIDTH, D_MODEL), lambda i: (0, 0)),
                  pl.BlockSpec((SSM_WIDTH, D_MODEL), lambda i: (ATTN_WIDTH // SSM_WIDTH, 0)),
                  pl.BlockSpec((CONV_WIDTH, D_MODEL), lambda i: ((ATTN_WIDTH + SSM_WIDTH) // CONV_WIDTH, 0)),
                  row(D_MODEL)],
        out_specs=row(D_MODEL),
        out_shape=jax.ShapeDtypeStruct((t, D_MODEL), F32),
        compiler_params=_params("parallel"),
        name="outproj",
    )(attn, ssm, conv, w_out, w_out, w_out, h)


def _ffn_kernel(h_ref, nw_ref, wg_ref, wu_ref, wd_ref, o_ref, hn_ref):
    @pl.when(pl.program_id(1) == 0)
    def _():
        h = h_ref[...]
        hn_ref[...] = _rms(h, nw_ref[...]).astype(BF16)
        o_ref[...] = h

    hn = hn_ref[...]
    a = _silu(_dot(hn, wg_ref[...])) * _dot(hn, wu_ref[...])
    o_ref[...] += _dot(a.astype(BF16), wd_ref[...])


def _ffn(h, nw, wg, wu, wd, tm=1024, tf=512):
    t = h.shape[0]
    return pl.pallas_call(
        _ffn_kernel,
        grid=(t // tm, D_FF // tf),
        in_specs=[pl.BlockSpec((tm, D_MODEL), lambda i, f: (i, 0)),
                  pl.BlockSpec((1, D_MODEL), lambda i, f: (0, 0)),
                  pl.BlockSpec((D_MODEL, tf), lambda i, f: (0, f)),
                  pl.BlockSpec((D_MODEL, tf), lambda i, f: (0, f)),
                  pl.BlockSpec((tf, D_MODEL), lambda i, f: (f, 0))],
        out_specs=pl.BlockSpec((tm, D_MODEL), lambda i, f: (i, 0)),
        out_shape=jax.ShapeDtypeStruct((t, D_MODEL), F32),
        scratch_shapes=[pltpu.VMEM((tm, D_MODEL), BF16)],
        compiler_params=_params("parallel", "arbitrary"),
        name="ffn",
    )(h, nw, wg, wu, wd)


TOK_CHUNK = 256
ROW_BLOCK = 256
MOE_TM = 1024


def _router_kernel(h_ref, nw_ref, wr_ref, hn_ref, comb_ref, rank_ref, rankt_ref, cnt_ref, carry_ref):
    @pl.when(pl.program_id(0) == 0)
    def _():
        carry_ref[...] = jnp.zeros_like(carry_ref)

    hn = _rms(h_ref[...], nw_ref[...])
    hn_ref[...] = hn.astype(BF16)
    logits = _dot_hi(hn, wr_ref[...])
    lane = lax.broadcasted_iota(jnp.int32, logits.shape, 1)
    logits = jnp.where(lane < N_EXPERTS, logits, -jnp.inf)
    m1 = jnp.max(logits, axis=1, keepdims=True)
    i1 = jnp.min(jnp.where(logits == m1, lane, LANES), axis=1, keepdims=True)
    rest = jnp.where(lane == i1, -jnp.inf, logits)
    m2 = jnp.max(rest, axis=1, keepdims=True)
    i2 = jnp.min(jnp.where(rest == m2, lane, LANES), axis=1, keepdims=True)
    e2 = jnp.exp(m2 - m1)
    den = 1.0 + e2
    comb_ref[...] = jnp.where(lane == i1, 1.0 / den, 0.0) + jnp.where(lane == i2, e2 / den, 0.0)
    sel = (lane == i1) | (lane == i2)
    self = jnp.where(sel, 1.0, 0.0)
    n = logits.shape[0]
    tri = jnp.where(lax.broadcasted_iota(jnp.int32, (n, n), 1) <= lax.broadcasted_iota(jnp.int32, (n, n), 0),
                    1.0, 0.0).astype(BF16)
    incl = _dot(tri, self.astype(BF16))
    carry = carry_ref[0:1, :]
    rank = jnp.where(sel, carry + incl - self, -1.0)
    rank_ref[...] = rank
    rankt_ref[...] = rank.T[:N_EXPERTS, :]
    carry_ref[...] = jnp.broadcast_to(carry + incl[n - 1:n, :], carry_ref.shape)
    cnt_ref[...] = carry_ref[...]


def _router(h, nw, wr):
    t = h.shape[0]
    nch = t // TOK_CHUNK
    wr_p = jnp.pad(wr, ((0, 0), (0, LANES - N_EXPERTS)))
    return pl.pallas_call(
        _router_kernel,
        grid=(nch,),
        in_specs=[pl.BlockSpec((TOK_CHUNK, D_MODEL), lambda i: (i, 0)),
                  pl.BlockSpec((1, D_MODEL), lambda i: (0, 0)),
                  pl.BlockSpec((D_MODEL, LANES), lambda i: (0, 0))],
        out_specs=[pl.BlockSpec((TOK_CHUNK, D_MODEL), lambda i: (i, 0)),
                   pl.BlockSpec((TOK_CHUNK, LANES), lambda i: (i, 0)),
                   pl.BlockSpec((TOK_CHUNK, LANES), lambda i: (i, 0)),
                   pl.BlockSpec((N_EXPERTS, TOK_CHUNK), lambda i: (0, i)),
                   pl.BlockSpec((None, 8, LANES), lambda i: (i, 0, 0))],
        out_shape=[jax.ShapeDtypeStruct((t, D_MODEL), BF16),
                   jax.ShapeDtypeStruct((t, LANES), F32),
                   jax.ShapeDtypeStruct((t, LANES), F32),
                   jax.ShapeDtypeStruct((N_EXPERTS, t), F32),
                   jax.ShapeDtypeStruct((nch, 8, LANES), F32)],
        scratch_shapes=[pltpu.VMEM((8, LANES), F32)],
        compiler_params=_params("arbitrary"),
        name="router",
    )(h, nw, wr_p)


def _moe_plan(cnt_after, t):
    nch = t // TOK_CHUNK
    n_tiles = (t * 2) // MOE_TM + N_EXPERTS
    sub = MOE_TM // ROW_BLOCK
    cnta = cnt_after[:, 0, :N_EXPERTS].astype(jnp.int32)
    cntb = jnp.concatenate([jnp.zeros((1, N_EXPERTS), jnp.int32), cnta[:-1]], axis=0)
    n = cnta[-1]
    tiles_e = (n + MOE_TM - 1) // MOE_TM
    tile_end = jnp.cumsum(tiles_e)
    tile_start = tile_end - tiles_e
    n_valid = tile_end[-1]
    m = jnp.arange(n_tiles, dtype=jnp.int32)
    last_e = jnp.searchsorted(tile_end, n_valid - 1, side='right').astype(jnp.int32)
    tile_e = jnp.where(m < n_valid, jnp.searchsorted(tile_end, m, side='right').astype(jnp.int32), last_e)
    tile_e = jnp.minimum(tile_e, N_EXPERTS - 1)
    rows_left = n[tile_e] - (m - tile_start[tile_e]) * MOE_TM
    tile_nsb = jnp.where(m < n_valid, jnp.clip((rows_left + ROW_BLOCK - 1) // ROW_BLOCK, 0, sub), 0).astype(jnp.int32)
    seg_row = (tile_start * MOE_TM).astype(jnp.int32)
    sb = jnp.arange(n_tiles * sub, dtype=jnp.int32)
    sb_e = tile_e[sb // sub]
    sb_r0 = sb * ROW_BLOCK - seg_row[sb_e]
    sb_valid = ((sb // sub) < n_valid) & (sb_r0 < n[sb_e])
    r1 = jnp.minimum(sb_r0 + ROW_BLOCK, n[sb_e])
    cnta_t, cntb_t = cnta.T, cntb.T
    lo = jax.vmap(lambda e, r: jnp.searchsorted(cnta_t[e], r, side='right'))(sb_e, sb_r0).astype(jnp.int32)
    hi = jax.vmap(lambda e, r: jnp.searchsorted(cntb_t[e], r, side='left'))(sb_e, r1).astype(jnp.int32) - 1
    sb_lo = jnp.where(sb_valid, lo, 0)
    sb_hi = jnp.where(sb_valid, hi, -1)
    comb_b0 = ((seg_row[None, :] + cntb) // ROW_BLOCK).astype(jnp.int32).reshape(-1)
    return dict(tile_e=tile_e, tile_nsb=tile_nsb, seg_row=seg_row, sb_e=sb_e, sb_r0=sb_r0.astype(jnp.int32),
                sb_lo=sb_lo, sb_hi=sb_hi, comb_b0=comb_b0, n_tiles=n_tiles)


def _dispatch_kernel(e_ref, r0_ref, lo_ref, hi_ref, hn_ref, comb_ref, rankt_ref, xs_ref, gs_ref, accx_ref, accg_ref,
                     *, nch):
    sb = pl.program_id(0)
    e = e_ref[sb]
    rowid = (lax.broadcasted_iota(jnp.int32, (ROW_BLOCK, TOK_CHUNK), 0) + r0_ref[sb]).astype(F32)
    accx_ref[...] = jnp.zeros_like(accx_ref)
    accg_ref[...] = jnp.zeros_like(accg_ref)

    def body(c, _):
        c0 = pl.multiple_of(c * TOK_CHUNK, TOK_CHUNK)
        p = rowid == rankt_ref[pl.ds(e * nch + c, 1), :]
        accx_ref[...] += _dot(jnp.where(p, 1.0, 0.0).astype(BF16), hn_ref[pl.ds(c0, TOK_CHUNK), :])
        accg_ref[...] += _dot_hi(jnp.where(p, 1.0, 0.0), comb_ref[pl.ds(c0, TOK_CHUNK), :])
        return 0

    lax.fori_loop(lo_ref[sb], hi_ref[sb] + 1, body, 0)
    xs_ref[...] = accx_ref[...].astype(BF16)
    lane = lax.broadcasted_iota(jnp.int32, accg_ref.shape, 1)
    g = jnp.sum(jnp.where(lane == e, accg_ref[...], 0.0), axis=1, keepdims=True)
    gs_ref[...] = jnp.broadcast_to(g, gs_ref.shape)


def _dispatch(plan, hn, comb, rankt):
    t = hn.shape[0]
    nch = t // TOK_CHUNK
    nsb = plan['n_tiles'] * (MOE_TM // ROW_BLOCK)
    whole = pl.BlockSpec(memory_space=pltpu.VMEM)
    return pl.pallas_call(
        functools.partial(_dispatch_kernel, nch=nch),
        grid_spec=pltpu.PrefetchScalarGridSpec(
            num_scalar_prefetch=4,
            grid=(nsb,),
            in_specs=[whole, whole, whole],
            out_specs=[pl.BlockSpec((ROW_BLOCK, D_MODEL), lambda s, *_: (s, 0)),
                       pl.BlockSpec((ROW_BLOCK, LANES), lambda s, *_: (s, 0))],
            scratch_shapes=[pltpu.VMEM((ROW_BLOCK, D_MODEL), F32), pltpu.VMEM((ROW_BLOCK, LANES), F32)]),
        out_shape=[jax.ShapeDtypeStruct((nsb * ROW_BLOCK, D_MODEL), BF16),
                   jax.ShapeDtypeStruct((nsb * ROW_BLOCK, LANES), F32)],
        compiler_params=_params("arbitrary"),
        name="dispatch",
    )(plan['sb_e'], plan['sb_r0'], plan['sb_lo'], plan['sb_hi'], hn, comb, rankt.reshape(N_EXPERTS * nch, TOK_CHUNK))


def _moe_kernel(te_ref, nsb_ref, x_ref, gs_ref, wg_ref, wu_ref, wd_ref, y_ref, acc_ref):
    m = pl.program_id(0)
    f = pl.program_id(1)
    nsb = nsb_ref[m]
    sub = MOE_TM // ROW_BLOCK

    @pl.when(f == 0)
    def _():
        acc_ref[...] = jnp.zeros_like(acc_ref)

    def block(rows):
        x = x_ref[rows, :]
        a = _silu(_dot(x, wg_ref[...])) * _dot(x, wu_ref[...])
        acc_ref[rows, :] += _dot(a.astype(BF16), wd_ref[...])

    @pl.when(nsb == sub)
    def _():
        block(slice(None))

    @pl.when((nsb > 0) & (nsb < sub))
    def _():
        def body(s, _):
            block(pl.ds(pl.multiple_of(s * ROW_BLOCK, ROW_BLOCK), ROW_BLOCK))
            return 0
        lax.fori_loop(0, nsb, body, 0)

    @pl.when(f == pl.num_programs(1) - 1)
    def _():
        y_ref[...] = (acc_ref[...] * gs_ref[:, 0:1]).astype(y_ref.dtype)


def _moe(plan, xs, gs, wg, wu, wd, tf=512):
    n_tiles = plan['n_tiles']
    nf = D_FF // tf

    def fsel(m, f, nsb):
        return jnp.where(nsb[m] > 0, f, nf - 1)

    return pl.pallas_call(
        _moe_kernel,
        grid_spec=pltpu.PrefetchScalarGridSpec(
            num_scalar_prefetch=2,
            grid=(n_tiles, nf),
            in_specs=[pl.BlockSpec((MOE_TM, D_MODEL), lambda m, f, te, nsb: (m, 0)),
                      pl.BlockSpec((MOE_TM, LANES), lambda m, f, te, nsb: (m, 0)),
                      pl.BlockSpec((None, D_MODEL, tf), lambda m, f, te, nsb: (te[m], 0, fsel(m, f, nsb))),
                      pl.BlockSpec((None, D_MODEL, tf), lambda m, f, te, nsb: (te[m], 0, fsel(m, f, nsb))),
                      pl.BlockSpec((None, tf, D_MODEL), lambda m, f, te, nsb: (te[m], fsel(m, f, nsb), 0))],
            out_specs=pl.BlockSpec((MOE_TM, D_MODEL), lambda m, f, te, nsb: (m, 0)),
            scratch_shapes=[pltpu.VMEM((MOE_TM, D_MODEL), F32)]),
        out_shape=jax.ShapeDtypeStruct((n_tiles * MOE_TM, D_MODEL), BF16),
        compiler_params=_params("arbitrary", "arbitrary"),
        name="moe",
    )(plan['tile_e'], plan['tile_nsb'], xs, gs, wg, wu, wd)


def _combine_kernel(b0_ref, seg_ref, h_ref, rank_ref, y0_ref, y1_ref, fw_ref, o_ref):
    c = pl.program_id(0)
    e = pl.program_id(1)

    @pl.when(e == 0)
    def _():
        o_ref[...] = h_ref[...]

    rk = rank_ref[...]
    lane = lax.broadcasted_iota(jnp.int32, rk.shape, 1)
    r = jnp.sum(jnp.where(lane == e, rk, 0.0), axis=1, keepdims=True)
    shift = (seg_ref[e] - b0_ref[c * N_EXPERTS + e] * ROW_BLOCK).astype(F32)
    loc = jnp.where(r < 0.0, -1.0, r + shift)
    col = lax.broadcasted_iota(jnp.int32, (TOK_CHUNK, ROW_BLOCK), 1).astype(F32)
    q0 = jnp.where(loc == col, 1.0, 0.0).astype(BF16)
    q1 = jnp.where(loc - ROW_BLOCK == col, 1.0, 0.0).astype(BF16)
    o_ref[...] += _dot(q0, y0_ref[...]) + _dot(q1, y1_ref[...])

    @pl.when(e == pl.num_programs(1) - 1)
    def _():
        o_ref[...] = _rms(o_ref[...], fw_ref[...])


def _combine(plan, h, rank, ys, fw):
    t = h.shape[0]
    nch = t // TOK_CHUNK
    nblk = ys.shape[0] // ROW_BLOCK
    return pl.pallas_call(
        _combine_kernel,
        grid_spec=pltpu.PrefetchScalarGridSpec(
            num_scalar_prefetch=2,
            grid=(nch, N_EXPERTS),
            in_specs=[pl.BlockSpec((TOK_CHUNK, D_MODEL), lambda c, e, b0, seg: (c, 0)),
                      pl.BlockSpec((TOK_CHUNK, LANES), lambda c, e, b0, seg: (c, 0)),
                      pl.BlockSpec((ROW_BLOCK, D_MODEL), lambda c, e, b0, seg: (b0[c * N_EXPERTS + e], 0)),
                      pl.BlockSpec((ROW_BLOCK, D_MODEL),
                                   lambda c, e, b0, seg: (jnp.minimum(b0[c * N_EXPERTS + e] + 1, nblk - 1), 0)),
                      pl.BlockSpec((1, D_MODEL), lambda c, e, b0, seg: (0, 0))],
            out_specs=pl.BlockSpec((TOK_CHUNK, D_MODEL), lambda c, e, b0, seg: (c, 0))),
        out_shape=jax.ShapeDtypeStruct((t, D_MODEL), F32),
        compiler_params=_params("arbitrary", "arbitrary"),
        name="combine",
    )(plan['comb_b0'], plan['seg_row'], h, rank, ys, ys, fw)


def _rope_tables(seqlen):
    pos = jnp.arange(seqlen, dtype=F32)
    inv_freq = 1.0 / (ROPE_THETA ** (jnp.arange(0, HEAD_DIM, 2, dtype=F32) / HEAD_DIM))
    ang = pos[:, None] * inv_freq[None, :]
    cos, sin = jnp.cos(ang), jnp.sin(ang)
    return jnp.concatenate([cos, cos], axis=-1), jnp.concatenate([-sin, sin], axis=-1)


def _arrange_w_in(w):
    q_k_v = w[:, :3 * ATTN_WIDTH]
    z = w[:, 3072:3584]
    xbc = w[:, 3584:4608]
    dt = jnp.pad(w[:, 4608:4616], ((0, 0), (0, LANES - SSM_HEADS)))
    glu = w[:, 4616:5640]
    return jnp.concatenate([q_k_v, xbc, glu, z, dt], axis=1).astype(BF16)


def kernel(x, norm_mix, w_in, ssm_conv_w, ssm_conv_b, ssm_dt_bias, ssm_a_log, ssm_d, ssm_norm_w, cf_conv_w, cf_conv_b, cf_ln_w, cf_ln_b, w_out, norm_ffn, ffn_w_gate, ffn_w_up, ffn_w_down, moe_router, moe_w_gate, moe_w_up, moe_w_down, norm_final):
    bsz, seqlen, _ = x.shape
    depth = w_in.shape[0]
    assert depth == 2 and ffn_w_gate.shape[0] == 1 and moe_router.shape[0] == 1
    cosf, sinf = _rope_tables(seqlen)
    h = x.reshape(bsz * seqlen, D_MODEL)
    for layer in range(depth):
        proj = _inproj(h, norm_mix[layer].reshape(1, -1), _arrange_w_in(w_in[layer]))
        attn = _moba(proj, cosf, sinf, bsz, seqlen)
        ssm = _ssd(proj, ssm_conv_w[layer], ssm_conv_b[layer], ssm_dt_bias[layer], ssm_a_log[layer],
                   ssm_d[layer], ssm_norm_w[layer], bsz, seqlen)
        conv = _conformer(proj, cf_conv_w[layer], cf_conv_b[layer], cf_ln_w[layer], cf_ln_b[layer], bsz, seqlen)
        h = _outproj(attn, ssm, conv, w_out[layer].astype(BF16), h)
        nw = norm_ffn[layer].reshape(1, -1)
        if layer % 2 == 0:
            i = layer // 2
            h = _ffn(h, nw, ffn_w_gate[i].astype(BF16), ffn_w_up[i].astype(BF16), ffn_w_down[i].astype(BF16))
        else:
            i = layer // 2
            hn, comb, rank, rankt, cnt_after = _router(h, nw, moe_router[i])
            plan = _moe_plan(cnt_after, h.shape[0])
            xs, gs = _dispatch(plan, hn, comb, rankt)
            ys = _moe(plan, xs, gs, moe_w_gate[i].astype(BF16), moe_w_up[i].astype(BF16), moe_w_down[i].astype(BF16))
            h = _combine(plan, h, rank, ys, norm_final.reshape(1, -1))
    return h.reshape(bsz, seqlen, D_MODEL)
```

```python
import functools
import math

import jax
import jax.numpy as jnp
from jax import lax
from jax.experimental import pallas as pl
from jax.experimental.pallas import tpu as pltpu

F32 = jnp.float32
BF16 = jnp.bfloat16

D_MODEL = 2048
HEAD_DIM = 128
ATTN_WIDTH = 1024
ATTN_HEADS = 8
MOBA_BLOCK = 256
MOBA_TOPK = 3
ROPE_THETA = 10000.0
SSM_WIDTH = 512
SSM_HEAD_DIM = 64
SSM_HEADS = 8
SSM_GROUPS = 2
SSM_STATE = 128
SSM_CONV = 4
SSM_CHUNK = 128
SSM_CONV_DIM = 1024
CONV_WIDTH = 512
CONV_SIZE = 31
D_FF = 7168
N_EXPERTS = 8
EPS = 1e-5

LANES = 128
NEG = -1e30

COL_Q, COL_K, COL_V = 0, 1024, 2048
COL_XBC, COL_GLU, COL_Z, COL_DT = 3072, 4096, 5120, 5632
IN_PAD = 5760

VMEM_LIMIT = 56 * 1024 * 1024


def _params(*sem):
    return pltpu.CompilerParams(dimension_semantics=sem, vmem_limit_bytes=VMEM_LIMIT)


def _dot(a, b):
    return jnp.dot(a, b, preferred_element_type=F32)


def _dot_nt(a, b):
    return lax.dot_general(a, b, (((1,), (1,)), ((), ())), preferred_element_type=F32)


def _dot_hi(a, b):
    return jnp.dot(a, b, preferred_element_type=F32, precision=lax.Precision.HIGHEST)


def _dot_nt_hi(a, b):
    return lax.dot_general(a, b, (((1,), (1,)), ((), ())), preferred_element_type=F32,
                           precision=lax.Precision.HIGHEST)


def _rms(x, w):
    return x * lax.rsqrt(jnp.mean(x * x, axis=-1, keepdims=True) + EPS) * w


def _silu(x):
    return x * jax.nn.sigmoid(x)


def _inproj_kernel(x_ref, nw_ref, w_ref, o_ref, xn_ref):
    @pl.when(pl.program_id(1) == 0)
    def _():
        xn_ref[...] = _rms(x_ref[...], nw_ref[...]).astype(BF16)

    o_ref[...] = _dot(xn_ref[...], w_ref[...])


def _inproj(h, nw, w, tm=1024, tn=1152):
    t = h.shape[0]
    return pl.pallas_call(
        _inproj_kernel,
        grid=(t // tm, IN_PAD // tn),
        in_specs=[pl.BlockSpec((tm, D_MODEL), lambda i, j: (i, 0)),
                  pl.BlockSpec((1, D_MODEL), lambda i, j: (0, 0)),
                  pl.BlockSpec((D_MODEL, tn), lambda i, j: (0, j))],
        out_specs=pl.BlockSpec((tm, tn), lambda i, j: (i, j)),
        out_shape=jax.ShapeDtypeStruct((t, IN_PAD), F32),
        scratch_shapes=[pltpu.VMEM((tm, D_MODEL), BF16)],
        compiler_params=_params("parallel", "arbitrary"),
        name="inproj",
    )(h, nw, w)


def _rope(x, cosf, sinf):
    return x * cosf + pltpu.roll(x, HEAD_DIM // 2, 1) * sinf


MOBA_HP = 2


def _moba_kernel(q_ref, k_ref, v_ref, cos_ref, sin_ref, o_ref, kr_ref, vb_ref, km_ref, *, nb):
    i = pl.program_id(2)
    blk = MOBA_BLOCK
    heads = [slice(hh * HEAD_DIM, (hh + 1) * HEAD_DIM) for hh in range(MOBA_HP)]

    @pl.when(i == 0)
    def _():
        km_ref[...] = jnp.zeros_like(km_ref)
        vb_ref[...] = v_ref[...].astype(BF16)
        for hh, hs in enumerate(heads):
            for j in range(nb):
                sl = slice(j * blk, (j + 1) * blk)
                kr = _rope(k_ref[sl, hs], cos_ref[sl, :], sin_ref[sl, :])
                kr_ref[sl, hs] = kr.astype(BF16)
                km_ref[hh, j:j + 1, :] = jnp.mean(kr, axis=0, keepdims=True)

    row0 = pl.multiple_of(i * blk, blk)
    rows = pl.ds(row0, blk)
    state, init = [], []
    for hh, hs in enumerate(heads):
        q = _rope(q_ref[:, hs], cos_ref[rows, :], sin_ref[rows, :])
        gate = _dot_nt_hi(q, km_ref[hh])
        lane = lax.broadcasted_iota(jnp.int32, gate.shape, 1)
        gate = jnp.where(lane < i, gate, -jnp.inf)
        cnt = jnp.zeros(gate.shape, jnp.int32)
        for c in range(nb - 1):
            gc = gate[:, c:c + 1]
            beats = (gc > gate) | ((gc == gate) & (c < lane))
            cnt = cnt + jnp.where(beats, 1, 0)
        selbias = jnp.where((cnt < MOBA_TOPK) & (lane < i), 0.0, NEG)

        qs = (q * (HEAD_DIM ** -0.5)).astype(BF16)
        s = _dot_nt(qs, kr_ref[rows, hs])
        qpos = lax.broadcasted_iota(jnp.int32, s.shape, 0)
        kpos = lax.broadcasted_iota(jnp.int32, s.shape, 1)
        s = jnp.where(kpos <= qpos, s, NEG)
        m0 = jnp.max(s, axis=1, keepdims=True)
        p = jnp.exp(s - m0)
        l0 = jnp.sum(p, axis=1, keepdims=True)
        acc0 = _dot(p.astype(BF16), vb_ref[rows, hs])
        state.append((hs, lane, selbias, qs))
        init.append((m0, l0, acc0))

    def body(j, carry):
        keys = pl.ds(pl.multiple_of(j * blk, blk), blk)
        out = []
        for (hs, lane, selbias, qs), (m, l, acc) in zip(state, carry):
            bias = jnp.max(jnp.where(lane == j, selbias, NEG), axis=1, keepdims=True)
            sj = _dot_nt(qs, kr_ref[keys, hs]) + bias
            mn = jnp.maximum(m, jnp.max(sj, axis=1, keepdims=True))
            alpha = jnp.exp(m - mn)
            pj = jnp.exp(sj - mn)
            l = alpha * l + jnp.sum(pj, axis=1, keepdims=True)
            acc = alpha * acc + _dot(pj.astype(BF16), vb_ref[keys, hs])
            out.append((mn, l, acc))
        return tuple(out)

    final = lax.fori_loop(0, i, body, tuple(init))
    for (hs, _, _, _), (_, l, acc) in zip(state, final):
        o_ref[:, hs] = (acc / l).astype(o_ref.dtype)


def _moba(proj, cosf, sinf, bsz, seqlen):
    nb = seqlen // MOBA_BLOCK
    hw = MOBA_HP * HEAD_DIM
    return pl.pallas_call(
        functools.partial(_moba_kernel, nb=nb),
        grid=(bsz, ATTN_HEADS // MOBA_HP, nb),
        in_specs=[pl.BlockSpec((MOBA_BLOCK, hw), lambda b, h, i: (b * nb + i, COL_Q // hw + h)),
                  pl.BlockSpec((seqlen, hw), lambda b, h, i: (b, COL_K // hw + h)),
                  pl.BlockSpec((seqlen, hw), lambda b, h, i: (b, COL_V // hw + h)),
                  pl.BlockSpec((seqlen, HEAD_DIM), lambda b, h, i: (0, 0)),
                  pl.BlockSpec((seqlen, HEAD_DIM), lambda b, h, i: (0, 0))],
        out_specs=pl.BlockSpec((MOBA_BLOCK, hw), lambda b, h, i: (b * nb + i, h)),
        out_shape=jax.ShapeDtypeStruct((bsz * seqlen, ATTN_WIDTH), BF16),
        scratch_shapes=[pltpu.VMEM((seqlen, hw), BF16),
                        pltpu.VMEM((seqlen, hw), BF16),
                        pltpu.VMEM((MOBA_HP, LANES, HEAD_DIM), F32)],
        compiler_params=_params("parallel", "parallel", "arbitrary"),
        name="moba",
    )(proj, proj, proj, cosf, sinf)


def _ssd_kernel(xbc_ref, z_ref, dt_ref, cw_ref, cb_ref, dtb_ref, alog_ref, dch_ref, nw_ref, ex_ref,
                o_ref, xpad_ref, st_ref, y_ref):
    c = pl.program_id(1)
    cl = SSM_CHUNK
    hd = SSM_HEAD_DIM
    gw = SSM_WIDTH // SSM_GROUPS
    hg = SSM_HEADS // SSM_GROUPS

    @pl.when(c == 0)
    def _():
        xpad_ref[0:8, :] = jnp.zeros((8, SSM_CONV_DIM), F32)
        st_ref[...] = jnp.zeros_like(st_ref)

    @pl.when(c > 0)
    def _():
        xpad_ref[0:8, :] = xpad_ref[cl:cl + 8, :]

    xpad_ref[8:8 + cl, :] = xbc_ref[...]
    conv = jnp.zeros((cl, SSM_CONV_DIM), F32) + cb_ref[...]
    for k in range(SSM_CONV):
        off = 8 - (SSM_CONV - 1) + k
        conv = conv + xpad_ref[off:off + cl, :] * cw_ref[k:k + 1, :]
    act = _silu(conv)
    xs = act[:, :SSM_WIDTH]

    dt = jax.nn.softplus(dt_ref[...] + dtb_ref[...])
    la = dt * (-jnp.exp(alog_ref[...]))
    ti = lax.broadcasted_iota(jnp.int32, (cl, cl), 0)
    si = lax.broadcasted_iota(jnp.int32, (cl, cl), 1)
    causal = si <= ti
    cum = _dot_hi(jnp.where(causal, 1.0, 0.0), la)
    cum_t = cum.T
    ex = ex_ref[...]
    dt_c = _dot_hi(dt, ex)
    cum_c = _dot_hi(cum, ex)
    cum_last = cum_c[cl - 1:cl, :]
    xr = xs * dt_c
    xw = xr * jnp.exp(cum_last - cum_c)
    e_cum = jnp.exp(cum_c)
    e_last = jnp.exp(cum_last)

    for g in range(SSM_GROUPS):
        bm = act[:, SSM_WIDTH + g * SSM_STATE:SSM_WIDTH + (g + 1) * SSM_STATE]
        cm = act[:, SSM_WIDTH + (SSM_GROUPS + g) * SSM_STATE:SSM_WIDTH + (SSM_GROUPS + g + 1) * SSM_STATE]
        bm_t = bm.T
        cb = _dot(cm, bm_t)
        gs = slice(g * gw, (g + 1) * gw)
        st = st_ref[g]
        y_ref[:, gs] = _dot(cm, st) * e_cum[:, gs]
        st_ref[g] = st * e_last[:, gs] + _dot(bm_t, xw[:, gs])
        for hh in range(hg):
            h = g * hg + hh
            seg = cum[:, h:h + 1] - cum_t[h:h + 1, :]
            decay = jnp.exp(jnp.where(causal, seg, -jnp.inf))
            hs = slice(h * hd, (h + 1) * hd)
            y_ref[:, hs] = y_ref[:, hs] + _dot(cb * decay, xr[:, hs])

    y = (y_ref[...] + dch_ref[...] * xs) * _silu(z_ref[...])
    for g in range(SSM_GROUPS):
        gs = slice(g * gw, (g + 1) * gw)
        yg = y[:, gs]
        yg = yg * lax.rsqrt(jnp.mean(yg * yg, axis=-1, keepdims=True) + EPS)
        o_ref[:, gs] = (yg * nw_ref[:, gs]).astype(o_ref.dtype)


def _ssd(proj, conv_w, conv_b, dt_bias, a_log, d_skip, norm_w, bsz, seqlen):
    nc = seqlen // SSM_CHUNK
    pad = LANES - SSM_HEADS
    dtb = jnp.pad(dt_bias, (0, pad)).reshape(1, LANES)
    alog = jnp.pad(a_log, (0, pad)).reshape(1, LANES)
    dch = jnp.repeat(d_skip, SSM_HEAD_DIM).reshape(1, SSM_WIDTH)
    expand = (jnp.arange(LANES)[:, None] == (jnp.arange(SSM_WIDTH)[None, :] // SSM_HEAD_DIM)).astype(F32)
    const = lambda shape: pl.BlockSpec(shape, lambda b, c: (0, 0))
    return pl.pallas_call(
        _ssd_kernel,
        grid=(bsz, nc),
        in_specs=[pl.BlockSpec((SSM_CHUNK, SSM_CONV_DIM), lambda b, c: (b * nc + c, COL_XBC // SSM_CONV_DIM)),
                  pl.BlockSpec((SSM_CHUNK, SSM_WIDTH), lambda b, c: (b * nc + c, COL_Z // SSM_WIDTH)),
                  pl.BlockSpec((SSM_CHUNK, LANES), lambda b, c: (b * nc + c, COL_DT // LANES)),
                  const((SSM_CONV, SSM_CONV_DIM)), const((1, SSM_CONV_DIM)),
                  const((1, LANES)), const((1, LANES)), const((1, SSM_WIDTH)), const((1, SSM_WIDTH)),
                  const((LANES, SSM_WIDTH))],
        out_specs=pl.BlockSpec((SSM_CHUNK, SSM_WIDTH), lambda b, c: (b * nc + c, 0)),
        out_shape=jax.ShapeDtypeStruct((bsz * seqlen, SSM_WIDTH), BF16),
        scratch_shapes=[pltpu.VMEM((SSM_CHUNK + 8, SSM_CONV_DIM), F32),
                        pltpu.VMEM((SSM_GROUPS, SSM_STATE, SSM_WIDTH // SSM_GROUPS), F32),
                        pltpu.VMEM((SSM_CHUNK, SSM_WIDTH), F32)],
        compiler_params=_params("parallel", "arbitrary"),
        name="ssd",
    )(proj, proj, proj, conv_w, conv_b.reshape(1, -1), dtb, alog, dch, norm_w.reshape(1, -1), expand)


CF_TILE = 256
CF_HALO = 32
CF_ROWS = 64


def _conformer_kernel(u_ref, cw_ref, cb_ref, lw_ref, lb_ref, o_ref, buf_ref, cv_ref):
    i = pl.program_id(1)
    ts = CF_TILE

    @pl.when(i == 0)
    def _():
        buf_ref[0:CF_HALO, :] = jnp.zeros((CF_HALO, CONV_WIDTH), F32)

    @pl.when(i > 0)
    def _():
        buf_ref[0:CF_HALO, :] = buf_ref[ts:ts + CF_HALO, :]

    buf_ref[CF_HALO:CF_HALO + ts, :] = u_ref[:, :CONV_WIDTH] * jax.nn.sigmoid(u_ref[:, CONV_WIDTH:])
    base = CF_HALO - (CONV_SIZE - 1)
    for r in range(ts // CF_ROWS):
        for cc in range(CONV_WIDTH // LANES):
            cs = slice(cc * LANES, (cc + 1) * LANES)
            acc = jnp.zeros((CF_ROWS, LANES), F32) + cb_ref[:, cs]
            for k in range(CONV_SIZE):
                off = r * CF_ROWS + base + k
                acc = acc + buf_ref[off:off + CF_ROWS, cs] * cw_ref[k:k + 1, cs]
            cv_ref[r * CF_ROWS:(r + 1) * CF_ROWS, cs] = acc
    hf = cv_ref[...]
    mu = jnp.mean(hf, axis=-1, keepdims=True)
    d = hf - mu
    var = jnp.mean(d * d, axis=-1, keepdims=True)
    o_ref[...] = _silu(d * lax.rsqrt(var + EPS) * lw_ref[...] + lb_ref[...]).astype(o_ref.dtype)


def _conformer(proj, conv_w, conv_b, ln_w, ln_b, bsz, seqlen):
    nt = seqlen // CF_TILE
    const = lambda shape: pl.BlockSpec(shape, lambda b, i: (0, 0))
    return pl.pallas_call(
        _conformer_kernel,
        grid=(bsz, nt),
        in_specs=[pl.BlockSpec((CF_TILE, 2 * CONV_WIDTH), lambda b, i: (b * nt + i, COL_GLU // (2 * CONV_WIDTH))),
                  const((CONV_SIZE, CONV_WIDTH)), const((1, CONV_WIDTH)),
                  const((1, CONV_WIDTH)), const((1, CONV_WIDTH))],
        out_specs=pl.BlockSpec((CF_TILE, CONV_WIDTH), lambda b, i: (b * nt + i, 0)),
        out_shape=jax.ShapeDtypeStruct((bsz * seqlen, CONV_WIDTH), BF16),
        scratch_shapes=[pltpu.VMEM((CF_HALO + CF_TILE, CONV_WIDTH), F32),
                        pltpu.VMEM((CF_TILE, CONV_WIDTH), F32)],
        compiler_params=_params("parallel", "arbitrary"),
        name="conformer",
    )(proj, conv_w, conv_b.reshape(1, -1), ln_w.reshape(1, -1), ln_b.reshape(1, -1))


def _outproj_kernel(a_ref, s_ref, c_ref, wa_ref, ws_ref, wc_ref, h_ref, o_ref):
    o_ref[...] = (h_ref[...] + _dot(a_ref[...], wa_ref[...]) + _dot(s_ref[...], ws_ref[...])
                  + _dot(c_ref[...], wc_ref[...]))


def _outproj(attn, ssm, conv, w_out, h, tm=512):
    t = h.shape[0]
    row = lambda w: pl.BlockSpec((tm, w), lambda i: (i, 0))
    return pl.pallas_call(
        _outproj_kernel,
        grid=(t // tm,),
        in_specs=[row(ATTN_WIDTH), row(SSM_WIDTH), row(CONV_WIDTH),
                  pl.BlockSpec((ATTN_WIDTH, D_MODEL), lambda i: (0, 0)),
                  pl.BlockSpec((SSM_WIDTH, D_MODEL), lambda i: (ATTN_WIDTH // SSM_WIDTH, 0)),
                  pl.BlockSpec((CONV_WIDTH, D_MODEL), lambda i: ((ATTN_WIDTH + SSM_WIDTH) // CONV_WIDTH, 0)),
                  row(D_MODEL)],
        out_specs=row(D_MODEL),
        out_shape=jax.ShapeDtypeStruct((t, D_MODEL), F32),
        compiler_params=_params("parallel"),
        name="outproj",
    )(attn, ssm, conv, w_out, w_out, w_out, h)


def _ffn_kernel(h_ref, nw_ref, wg_ref, wu_ref, wd_ref, o_ref, hn_ref):
    @pl.when(pl.program_id(1) == 0)
    def _():
        h = h_ref[...]
        hn_ref[...] = _rms(h, nw_ref[...]).astype(BF16)
        o_ref[...] = h

    hn = hn_ref[...]
    a = _silu(_dot(hn, wg_ref[...])) * _dot(hn, wu_ref[...])
    o_ref[...] += _dot(a.astype(BF16), wd_ref[...])


def _ffn(h, nw, wg, wu, wd, tm=1024, tf=512):
    t = h.shape[0]
    return pl.pallas_call(
        _ffn_kernel,
        grid=(t // tm, D_FF // tf),
        in_specs=[pl.BlockSpec((tm, D_MODEL), lambda i, f: (i, 0)),
                  pl.BlockSpec((1, D_MODEL), lambda i, f: (0, 0)),
                  pl.BlockSpec((D_MODEL, tf), lambda i, f: (0, f)),
                  pl.BlockSpec((D_MODEL, tf), lambda i, f: (0, f)),
                  pl.BlockSpec((tf, D_MODEL), lambda i, f: (f, 0))],
        out_specs=pl.BlockSpec((tm, D_MODEL), lambda i, f: (i, 0)),
        out_shape=jax.ShapeDtypeStruct((t, D_MODEL), F32),
        scratch_shapes=[pltpu.VMEM((tm, D_MODEL), BF16)],
        compiler_params=_params("parallel", "arbitrary"),
        name="ffn",
    )(h, nw, wg, wu, wd)


TOK_CHUNK = 256
ROW_BLOCK = 256
MOE_TM = 1024


def _router_kernel(h_ref, nw_ref, wr_ref, hn_ref, comb_ref, rank_ref, rankt_ref, cnt_ref, carry_ref):
    @pl.when(pl.program_id(0) == 0)
    def _():
        carry_ref[...] = jnp.zeros_like(carry_ref)

    hn = _rms(h_ref[...], nw_ref[...])
    hn_ref[...] = hn.astype(BF16)
    logits = _dot_hi(hn, wr_ref[...])
    lane = lax.broadcasted_iota(jnp.int32, logits.shape, 1)
    logits = jnp.where(lane < N_EXPERTS, logits, -jnp.inf)
    m1 = jnp.max(logits, axis=1, keepdims=True)
    i1 = jnp.min(jnp.where(logits == m1, lane, LANES), axis=1, keepdims=True)
    rest = jnp.where(lane == i1, -jnp.inf, logits)
    m2 = jnp.max(rest, axis=1, keepdims=True)
    i2 = jnp.min(jnp.where(rest == m2, lane, LANES), axis=1, keepdims=True)
    e2 = jnp.exp(m2 - m1)
    den = 1.0 + e2
    comb_ref[...] = jnp.where(lane == i1, 1.0 / den, 0.0) + jnp.where(lane == i2, e2 / den, 0.0)
    sel = (lane == i1) | (lane == i2)
    self = jnp.where(sel, 1.0, 0.0)
    n = logits.shape[0]
    tri = jnp.where(lax.broadcasted_iota(jnp.int32, (n, n), 1) <= lax.broadcasted_iota(jnp.int32, (n, n), 0),
                    1.0, 0.0).astype(BF16)
    incl = _dot(tri, self.astype(BF16))
    carry = carry_ref[0:1, :]
    rank = jnp.where(sel, carry + incl - self, -1.0)
    rank_ref[...] = rank
    rankt_ref[...] = rank.T[:N_EXPERTS, :]
    carry_ref[...] = jnp.broadcast_to(carry + incl[n - 1:n, :], carry_ref.shape)
    cnt_ref[...] = carry_ref[...]


def _router(h, nw, wr):
    t = h.shape[0]
    nch = t // TOK_CHUNK
    wr_p = jnp.pad(wr, ((0, 0), (0, LANES - N_EXPERTS)))
    return pl.pallas_call(
        _router_kernel,
        grid=(nch,),
        in_specs=[pl.BlockSpec((TOK_CHUNK, D_MODEL), lambda i: (i, 0)),
                  pl.BlockSpec((1, D_MODEL), lambda i: (0, 0)),
                  pl.BlockSpec((D_MODEL, LANES), lambda i: (0, 0))],
        out_specs=[pl.BlockSpec((TOK_CHUNK, D_MODEL), lambda i: (i, 0)),
                   pl.BlockSpec((TOK_CHUNK, LANES), lambda i: (i, 0)),
                   pl.BlockSpec((TOK_CHUNK, LANES), lambda i: (i, 0)),
                   pl.BlockSpec((N_EXPERTS, TOK_CHUNK), lambda i: (0, i)),
                   pl.BlockSpec((None, 8, LANES), lambda i: (i, 0, 0))],
        out_shape=[jax.ShapeDtypeStruct((t, D_MODEL), BF16),
                   jax.ShapeDtypeStruct((t, LANES), F32),
                   jax.ShapeDtypeStruct((t, LANES), F32),
                   jax.ShapeDtypeStruct((N_EXPERTS, t), F32),
                   jax.ShapeDtypeStruct((nch, 8, LANES), F32)],
        scratch_shapes=[pltpu.VMEM((8, LANES), F32)],
        compiler_params=_params("arbitrary"),
        name="router",
    )(h, nw, wr_p)


def _moe_plan(cnt_after, t):
    nch = t // TOK_CHUNK
    n_tiles = (t * 2) // MOE_TM + N_EXPERTS
    sub = MOE_TM // ROW_BLOCK
    cnta = cnt_after[:, 0, :N_EXPERTS].astype(jnp.int32)
    cntb = jnp.concatenate([jnp.zeros((1, N_EXPERTS), jnp.int32), cnta[:-1]], axis=0)
    n = cnta[-1]
    tiles_e = (n + MOE_TM - 1) // MOE_TM
    tile_end = jnp.cumsum(tiles_e)
    tile_start = tile_end - tiles_e
    n_valid = tile_end[-1]
    eid = jnp.arange(N_EXPERTS, dtype=jnp.int32)

    def count(cond, axis):
        return jnp.sum(cond.astype(jnp.int32), axis=axis)

    def take(table, idx):
        hot = (idx[:, None] == eid[None, :]).astype(jnp.int32)
        if table.ndim == 1:
            return jnp.sum(hot * table[None, :], axis=1)
        return jnp.sum(hot[:, :, None] * table[None, :, :], axis=1)

    m = jnp.arange(n_tiles, dtype=jnp.int32)
    last_e = count(tile_end <= n_valid - 1, 0)
    tile_e = jnp.where(m < n_valid, count(tile_end[None, :] <= m[:, None], 1), last_e)
    tile_e = jnp.minimum(tile_e, N_EXPERTS - 1).astype(jnp.int32)
    rows_left = take(n, tile_e) - (m - take(tile_start, tile_e)) * MOE_TM
    tile_nsb = jnp.where(m < n_valid, jnp.clip((rows_left + ROW_BLOCK - 1) // ROW_BLOCK, 0, sub), 0).astype(jnp.int32)
    seg_row = (tile_start * MOE_TM).astype(jnp.int32)
    sb = jnp.arange(n_tiles * sub, dtype=jnp.int32)
    sb_e = jnp.repeat(tile_e, sub)
    sb_n = take(n, sb_e)
    sb_r0 = sb * ROW_BLOCK - take(seg_row, sb_e)
    sb_valid = ((sb // sub) < n_valid) & (sb_r0 < sb_n)
    r1 = jnp.minimum(sb_r0 + ROW_BLOCK, sb_n)
    lo = count(take(cnta.T, sb_e) <= sb_r0[:, None], 1)
    hi = count(take(cntb.T, sb_e) < r1[:, None], 1) - 1
    sb_lo = jnp.where(sb_valid, lo, 0)
    sb_hi = jnp.where(sb_valid, hi, -1)
    comb_b0 = ((seg_row[None, :] + cntb) // ROW_BLOCK).astype(jnp.int32).reshape(-1)
    return dict(tile_e=tile_e, tile_nsb=tile_nsb, seg_row=seg_row, sb_e=sb_e, sb_r0=sb_r0.astype(jnp.int32),
                sb_lo=sb_lo, sb_hi=sb_hi, comb_b0=comb_b0, n_tiles=n_tiles)


def _dispatch_kernel(e_ref, r0_ref, lo_ref, hi_ref, hn_ref, comb_ref, rankt_ref, xs_ref, gs_ref, accx_ref, accg_ref,
                     *, nch):
    sb = pl.program_id(0)
    e = e_ref[sb]
    rowid = (lax.broadcasted_iota(jnp.int32, (ROW_BLOCK, TOK_CHUNK), 0) + r0_ref[sb]).astype(F32)
    accx_ref[...] = jnp.zeros_like(accx_ref)
    accg_ref[...] = jnp.zeros_like(accg_ref)

    def body(c, _):
        c0 = pl.multiple_of(c * TOK_CHUNK, TOK_CHUNK)
        p = rowid == rankt_ref[pl.ds(e * nch + c, 1), :]
        accx_ref[...] += _dot(jnp.where(p, 1.0, 0.0).astype(BF16), hn_ref[pl.ds(c0, TOK_CHUNK), :])
        accg_ref[...] += _dot_hi(jnp.where(p, 1.0, 0.0), comb_ref[pl.ds(c0, TOK_CHUNK), :])
        return 0

    lax.fori_loop(lo_ref[sb], hi_ref[sb] + 1, body, 0)
    xs_ref[...] = accx_ref[...].astype(BF16)
    lane = lax.broadcasted_iota(jnp.int32, accg_ref.shape, 1)
    g = jnp.sum(jnp.where(lane == e, accg_ref[...], 0.0), axis=1, keepdims=True)
    gs_ref[...] = jnp.broadcast_to(g, gs_ref.shape)


def _dispatch(plan, hn, comb, rankt):
    t = hn.shape[0]
    nch = t // TOK_CHUNK
    nsb = plan['n_tiles'] * (MOE_TM // ROW_BLOCK)
    whole = pl.BlockSpec(memory_space=pltpu.VMEM)
    return pl.pallas_call(
        functools.partial(_dispatch_kernel, nch=nch),
        grid_spec=pltpu.PrefetchScalarGridSpec(
            num_scalar_prefetch=4,
            grid=(nsb,),
            in_specs=[whole, whole, whole],
            out_specs=[pl.BlockSpec((ROW_BLOCK, D_MODEL), lambda s, *_: (s, 0)),
                       pl.BlockSpec((ROW_BLOCK, LANES), lambda s, *_: (s, 0))],
            scratch_shapes=[pltpu.VMEM((ROW_BLOCK, D_MODEL), F32), pltpu.VMEM((ROW_BLOCK, LANES), F32)]),
        out_shape=[jax.ShapeDtypeStruct((nsb * ROW_BLOCK, D_MODEL), BF16),
                   jax.ShapeDtypeStruct((nsb * ROW_BLOCK, LANES), F32)],
        compiler_params=_params("arbitrary"),
        name="dispatch",
    )(plan['sb_e'], plan['sb_r0'], plan['sb_lo'], plan['sb_hi'], hn, comb, rankt.reshape(N_EXPERTS * nch, TOK_CHUNK))


def _moe_kernel(te_ref, nsb_ref, x_ref, gs_ref, wg_ref, wu_ref, wd_ref, y_ref, acc_ref):
    m = pl.program_id(0)
    f = pl.program_id(1)
    nsb = nsb_ref[m]
    sub = MOE_TM // ROW_BLOCK

    @pl.when(f == 0)
    def _():
        acc_ref[...] = jnp.zeros_like(acc_ref)

    def block(rows):
        x = x_ref[rows, :]
        a = _silu(_dot(x, wg_ref[...])) * _dot(x, wu_ref[...])
        acc_ref[rows, :] += _dot(a.astype(BF16), wd_ref[...])

    @pl.when(nsb == sub)
    def _():
        block(slice(None))

    @pl.when((nsb > 0) & (nsb < sub))
    def _():
        def body(s, _):
            block(pl.ds(pl.multiple_of(s * ROW_BLOCK, ROW_BLOCK), ROW_BLOCK))
            return 0
        lax.fori_loop(0, nsb, body, 0)

    @pl.when(f == pl.num_programs(1) - 1)
    def _():
        y_ref[...] = (acc_ref[...] * gs_ref[:, 0:1]).astype(y_ref.dtype)


def _moe(plan, xs, gs, wg, wu, wd, tf=512):
    n_tiles = plan['n_tiles']
    nf = D_FF // tf

    def fsel(m, f, nsb):
        return jnp.where(nsb[m] > 0, f, nf - 1)

    return pl.pallas_call(
        _moe_kernel,
        grid_spec=pltpu.PrefetchScalarGridSpec(
            num_scalar_prefetch=2,
            grid=(n_tiles, nf),
            in_specs=[pl.BlockSpec((MOE_TM, D_MODEL), lambda m, f, te, nsb: (m, 0)),
                      pl.BlockSpec((MOE_TM, LANES), lambda m, f, te, nsb: (m, 0)),
                      pl.BlockSpec((None, D_MODEL, tf), lambda m, f, te, nsb: (te[m], 0, fsel(m, f, nsb))),
                      pl.BlockSpec((None, D_MODEL, tf), lambda m, f, te, nsb: (te[m], 0, fsel(m, f, nsb))),
                      pl.BlockSpec((None, tf, D_MODEL), lambda m, f, te, nsb: (te[m], fsel(m, f, nsb), 0))],
            out_specs=pl.BlockSpec((MOE_TM, D_MODEL), lambda m, f, te, nsb: (m, 0)),
            scratch_shapes=[pltpu.VMEM((MOE_TM, D_MODEL), F32)]),
        out_shape=jax.ShapeDtypeStruct((n_tiles * MOE_TM, D_MODEL), BF16),
        compiler_params=_params("arbitrary", "arbitrary"),
        name="moe",
    )(plan['tile_e'], plan['tile_nsb'], xs, gs, wg, wu, wd)


def _combine_kernel(b0_ref, seg_ref, h_ref, rank_ref, y0_ref, y1_ref, fw_ref, o_ref):
    c = pl.program_id(0)
    e = pl.program_id(1)

    @pl.when(e == 0)
    def _():
        o_ref[...] = h_ref[...]

    rk = rank_ref[...]
    lane = lax.broadcasted_iota(jnp.int32, rk.shape, 1)
    r = jnp.sum(jnp.where(lane == e, rk, 0.0), axis=1, keepdims=True)
    shift = (seg_ref[e] - b0_ref[c * N_EXPERTS + e] * ROW_BLOCK).astype(F32)
    loc = jnp.where(r < 0.0, -1.0, r + shift)
    col = lax.broadcasted_iota(jnp.int32, (TOK_CHUNK, ROW_BLOCK), 1).astype(F32)
    q0 = jnp.where(loc == col, 1.0, 0.0).astype(BF16)
    q1 = jnp.where(loc - ROW_BLOCK == col, 1.0, 0.0).astype(BF16)
    o_ref[...] += _dot(q0, y0_ref[...]) + _dot(q1, y1_ref[...])

    @pl.when(e == pl.num_programs(1) - 1)
    def _():
        o_ref[...] = _rms(o_ref[...], fw_ref[...])


def _combine(plan, h, rank, ys, fw):
    t = h.shape[0]
    nch = t // TOK_CHUNK
    nblk = ys.shape[0] // ROW_BLOCK
    return pl.pallas_call(
        _combine_kernel,
        grid_spec=pltpu.PrefetchScalarGridSpec(
            num_scalar_prefetch=2,
            grid=(nch, N_EXPERTS),
            in_specs=[pl.BlockSpec((TOK_CHUNK, D_MODEL), lambda c, e, b0, seg: (c, 0)),
                      pl.BlockSpec((TOK_CHUNK, LANES), lambda c, e, b0, seg: (c, 0)),
                      pl.BlockSpec((ROW_BLOCK, D_MODEL), lambda c, e, b0, seg: (b0[c * N_EXPERTS + e], 0)),
                      pl.BlockSpec((ROW_BLOCK, D_MODEL),
                                   lambda c, e, b0, seg: (jnp.minimum(b0[c * N_EXPERTS + e] + 1, nblk - 1), 0)),
                      pl.BlockSpec((1, D_MODEL), lambda c, e, b0, seg: (0, 0))],
            out_specs=pl.BlockSpec((TOK_CHUNK, D_MODEL), lambda c, e, b0, seg: (c, 0))),
        out_shape=jax.ShapeDtypeStruct((t, D_MODEL), F32),
        compiler_params=_params("arbitrary", "arbitrary"),
        name="combine",
    )(plan['comb_b0'], plan['seg_row'], h, rank, ys, ys, fw)


def _rope_tables(seqlen):
    pos = jnp.arange(seqlen, dtype=F32)
    inv_freq = 1.0 / (ROPE_THETA ** (jnp.arange(0, HEAD_DIM, 2, dtype=F32) / HEAD_DIM))
    ang = pos[:, None] * inv_freq[None, :]
    cos, sin = jnp.cos(ang), jnp.sin(ang)
    return jnp.concatenate([cos, cos], axis=-1), jnp.concatenate([-sin, sin], axis=-1)


def _arrange_w_in(w):
    q_k_v = w[:, :3 * ATTN_WIDTH]
    z = w[:, 3072:3584]
    xbc = w[:, 3584:4608]
    dt = jnp.pad(w[:, 4608:4616], ((0, 0), (0, LANES - SSM_HEADS)))
    glu = w[:, 4616:5640]
    return jnp.concatenate([q_k_v, xbc, glu, z, dt], axis=1).astype(BF16)


def kernel(x, norm_mix, w_in, ssm_conv_w, ssm_conv_b, ssm_dt_bias, ssm_a_log, ssm_d, ssm_norm_w, cf_conv_w, cf_conv_b, cf_ln_w, cf_ln_b, w_out, norm_ffn, ffn_w_gate, ffn_w_up, ffn_w_down, moe_router, moe_w_gate, moe_w_up, moe_w_down, norm_final):
    bsz, seqlen, _ = x.shape
    depth = w_in.shape[0]
    assert depth == 2 and ffn_w_gate.shape[0] == 1 and moe_router.shape[0] == 1
    cosf, sinf = _rope_tables(seqlen)
    h = x.reshape(bsz * seqlen, D_MODEL)
    for layer in range(depth):
        proj = _inproj(h, norm_mix[layer].reshape(1, -1), _arrange_w_in(w_in[layer]))
        attn = _moba(proj, cosf, sinf, bsz, seqlen)
        ssm = _ssd(proj, ssm_conv_w[layer], ssm_conv_b[layer], ssm_dt_bias[layer], ssm_a_log[layer],
                   ssm_d[layer], ssm_norm_w[layer], bsz, seqlen)
        conv = _conformer(proj, cf_conv_w[layer], cf_conv_b[layer], cf_ln_w[layer], cf_ln_b[layer], bsz, seqlen)
        h = _outproj(attn, ssm, conv, w_out[layer].astype(BF16), h)
        nw = norm_ffn[layer].reshape(1, -1)
        if layer % 2 == 0:
            i = layer // 2
            h = _ffn(h, nw, ffn_w_gate[i].astype(BF16), ffn_w_up[i].astype(BF16), ffn_w_down[i].astype(BF16))
        else:
            i = layer // 2
            hn, comb, rank, rankt, cnt_after = _router(h, nw, moe_router[i])
            plan = _moe_plan(cnt_after, h.shape[0])
            xs, gs = _dispatch(plan, hn, comb, rankt)
            ys = _moe(plan, xs, gs, moe_w_gate[i].astype(BF16), moe_w_up[i].astype(BF16), moe_w_down[i].astype(BF16))
            h = _combine(plan, h, rank, ys, norm_final.reshape(1, -1))
    return h.reshape(bsz, seqlen, D_MODEL)
```

```python
import functools
import math

import jax
import jax.numpy as jnp
from jax import lax
from jax.experimental import pallas as pl
from jax.experimental.pallas import tpu as pltpu

F32 = jnp.float32
BF16 = jnp.bfloat16

D_MODEL = 2048
HEAD_DIM = 128
ATTN_WIDTH = 1024
ATTN_HEADS = 8
MOBA_BLOCK = 256
MOBA_TOPK = 3
ROPE_THETA = 10000.0
SSM_WIDTH = 512
SSM_HEAD_DIM = 64
SSM_HEADS = 8
SSM_GROUPS = 2
SSM_STATE = 128
SSM_CONV = 4
SSM_CHUNK = 128
SSM_CONV_DIM = 1024
CONV_WIDTH = 512
CONV_SIZE = 31
D_FF = 7168
N_EXPERTS = 8
EPS = 1e-5

LANES = 128
NEG = -1e30

COL_Q, COL_K, COL_V = 0, 1024, 2048
COL_XBC, COL_GLU, COL_Z, COL_DT = 3072, 4096, 5120, 5632
IN_PAD = 5760

VMEM_LIMIT = 56 * 1024 * 1024


def _params(*sem):
    return pltpu.CompilerParams(dimension_semantics=sem, vmem_limit_bytes=VMEM_LIMIT)


def _dot(a, b):
    return jnp.dot(a, b, preferred_element_type=F32)


def _dot_nt(a, b):
    return lax.dot_general(a, b, (((1,), (1,)), ((), ())), preferred_element_type=F32)


def _dot_hi(a, b):
    return jnp.dot(a, b, preferred_element_type=F32, precision=lax.Precision.HIGHEST)


def _dot_nt_hi(a, b):
    return lax.dot_general(a, b, (((1,), (1,)), ((), ())), preferred_element_type=F32,
                           precision=lax.Precision.HIGHEST)


def _rms(x, w):
    return x * lax.rsqrt(jnp.mean(x * x, axis=-1, keepdims=True) + EPS) * w


def _silu(x):
    return x * jax.nn.sigmoid(x)


def _inproj_kernel(x_ref, nw_ref, w_ref, o_ref, xn_ref):
    @pl.when(pl.program_id(1) == 0)
    def _():
        xn_ref[...] = _rms(x_ref[...], nw_ref[...]).astype(BF16)

    o_ref[...] = _dot(xn_ref[...], w_ref[...])


def _inproj(h, nw, w, tm=1024, tn=1152):
    t = h.shape[0]
    return pl.pallas_call(
        _inproj_kernel,
        grid=(t // tm, IN_PAD // tn),
        in_specs=[pl.BlockSpec((tm, D_MODEL), lambda i, j: (i, 0)),
                  pl.BlockSpec((1, D_MODEL), lambda i, j: (0, 0)),
                  pl.BlockSpec((D_MODEL, tn), lambda i, j: (0, j))],
        out_specs=pl.BlockSpec((tm, tn), lambda i, j: (i, j)),
        out_shape=jax.ShapeDtypeStruct((t, IN_PAD), F32),
        scratch_shapes=[pltpu.VMEM((tm, D_MODEL), BF16)],
        compiler_params=_params("parallel", "arbitrary"),
        name="inproj",
    )(h, nw, w)


def _rope(x, cosf, sinf):
    return x * cosf + pltpu.roll(x, HEAD_DIM // 2, 1) * sinf


MOBA_HP = 2


def _moba_kernel(q_ref, k_ref, v_ref, cos_ref, sin_ref, o_ref, kr_ref, vb_ref, km_ref, *, nb):
    i = pl.program_id(2)
    blk = MOBA_BLOCK
    heads = [slice(hh * HEAD_DIM, (hh + 1) * HEAD_DIM) for hh in range(MOBA_HP)]

    @pl.when(i == 0)
    def _():
        km_ref[...] = jnp.zeros_like(km_ref)
        vb_ref[...] = v_ref[...].astype(BF16)
        for hh, hs in enumerate(heads):
            for j in range(nb):
                sl = slice(j * blk, (j + 1) * blk)
                kr = _rope(k_ref[sl, hs], cos_ref[sl, :], sin_ref[sl, :])
                kr_ref[sl, hs] = kr.astype(BF16)
                km_ref[hh, j:j + 1, :] = jnp.mean(kr, axis=0, keepdims=True)

    row0 = pl.multiple_of(i * blk, blk)
    rows = pl.ds(row0, blk)
    state, init = [], []
    for hh, hs in enumerate(heads):
        q = _rope(q_ref[:, hs], cos_ref[rows, :], sin_ref[rows, :])
        gate = _dot_nt_hi(q, km_ref[hh])
        lane = lax.broadcasted_iota(jnp.int32, gate.shape, 1)
        gate = jnp.where(lane < i, gate, -jnp.inf)
        cnt = jnp.zeros(gate.shape, jnp.int32)
        for c in range(nb - 1):
            gc = gate[:, c:c + 1]
            beats = (gc > gate) | ((gc == gate) & (c < lane))
            cnt = cnt + jnp.where(beats, 1, 0)
        selbias = jnp.where((cnt < MOBA_TOPK) & (lane < i), 0.0, NEG)

        qs = (q * (HEAD_DIM ** -0.5)).astype(BF16)
        s = _dot_nt(qs, kr_ref[rows, hs])
        qpos = lax.broadcasted_iota(jnp.int32, s.shape, 0)
        kpos = lax.broadcasted_iota(jnp.int32, s.shape, 1)
        s = jnp.where(kpos <= qpos, s, NEG)
        m0 = jnp.max(s, axis=1, keepdims=True)
        p = jnp.exp(s - m0)
        l0 = jnp.sum(p, axis=1, keepdims=True)
        acc0 = _dot(p.astype(BF16), vb_ref[rows, hs])
        state.append((hs, lane, selbias, qs))
        init.append((m0, l0, acc0))

    def body(j, carry):
        keys = pl.ds(pl.multiple_of(j * blk, blk), blk)
        out = []
        for (hs, lane, selbias, qs), (m, l, acc) in zip(state, carry):
            bias = jnp.max(jnp.where(lane == j, selbias, NEG), axis=1, keepdims=True)
            sj = _dot_nt(qs, kr_ref[keys, hs]) + bias
            mn = jnp.maximum(m, jnp.max(sj, axis=1, keepdims=True))
            alpha = jnp.exp(m - mn)
            pj = jnp.exp(sj - mn)
            l = alpha * l + jnp.sum(pj, axis=1, keepdims=True)
            acc = alpha * acc + _dot(pj.astype(BF16), vb_ref[keys, hs])
            out.append((mn, l, acc))
        return tuple(out)

    final = lax.fori_loop(0, i, body, tuple(init))
    for (hs, _, _, _), (_, l, acc) in zip(state, final):
        o_ref[:, hs] = (acc / l).astype(o_ref.dtype)


def _moba(proj, cosf, sinf, bsz, seqlen):
    nb = seqlen // MOBA_BLOCK
    hw = MOBA_HP * HEAD_DIM
    return pl.pallas_call(
        functools.partial(_moba_kernel, nb=nb),
        grid=(bsz, ATTN_HEADS // MOBA_HP, nb),
        in_specs=[pl.BlockSpec((MOBA_BLOCK, hw), lambda b, h, i: (b * nb + i, COL_Q // hw + h)),
                  pl.BlockSpec((seqlen, hw), lambda b, h, i: (b, COL_K // hw + h)),
                  pl.BlockSpec((seqlen, hw), lambda b, h, i: (b, COL_V // hw + h)),
                  pl.BlockSpec((seqlen, HEAD_DIM), lambda b, h, i: (0, 0)),
                  pl.BlockSpec((seqlen, HEAD_DIM), lambda b, h, i: (0, 0))],
        out_specs=pl.BlockSpec((MOBA_BLOCK, hw), lambda b, h, i: (b * nb + i, h)),
        out_shape=jax.ShapeDtypeStruct((bsz * seqlen, ATTN_WIDTH), BF16),
        scratch_shapes=[pltpu.VMEM((seqlen, hw), BF16),
                        pltpu.VMEM((seqlen, hw), BF16),
                        pltpu.VMEM((MOBA_HP, LANES, HEAD_DIM), F32)],
        compiler_params=_params("parallel", "parallel", "arbitrary"),
        name="moba",
    )(proj, proj, proj, cosf, sinf)


def _ssd_kernel(xbc_ref, z_ref, dt_ref, cw_ref, cb_ref, dtb_ref, alog_ref, dch_ref, nw_ref, ex_ref,
                o_ref, xpad_ref, st_ref, y_ref):
    c = pl.program_id(1)
    cl = SSM_CHUNK
    hd = SSM_HEAD_DIM
    gw = SSM_WIDTH // SSM_GROUPS
    hg = SSM_HEADS // SSM_GROUPS

    @pl.when(c == 0)
    def _():
        xpad_ref[0:8, :] = jnp.zeros((8, SSM_CONV_DIM), F32)
        st_ref[...] = jnp.zeros_like(st_ref)

    @pl.when(c > 0)
    def _():
        xpad_ref[0:8, :] = xpad_ref[cl:cl + 8, :]

    xpad_ref[8:8 + cl, :] = xbc_ref[...]
    conv = jnp.zeros((cl, SSM_CONV_DIM), F32) + cb_ref[...]
    for k in range(SSM_CONV):
        off = 8 - (SSM_CONV - 1) + k
        conv = conv + xpad_ref[off:off + cl, :] * cw_ref[k:k + 1, :]
    act = _silu(conv)
    xs = act[:, :SSM_WIDTH]

    dt = jax.nn.softplus(dt_ref[...] + dtb_ref[...])
    la = dt * (-jnp.exp(alog_ref[...]))
    ti = lax.broadcasted_iota(jnp.int32, (cl, cl), 0)
    si = lax.broadcasted_iota(jnp.int32, (cl, cl), 1)
    causal = si <= ti
    cum = _dot_hi(jnp.where(causal, 1.0, 0.0), la)
    cum_t = cum.T
    ex = ex_ref[...]
    dt_c = _dot_hi(dt, ex)
    cum_c = _dot_hi(cum, ex)
    cum_last = cum_c[cl - 1:cl, :]
    xr = xs * dt_c
    xw = xr * jnp.exp(cum_last - cum_c)
    e_cum = jnp.exp(cum_c)
    e_last = jnp.exp(cum_last)

    for g in range(SSM_GROUPS):
        bm = act[:, SSM_WIDTH + g * SSM_STATE:SSM_WIDTH + (g + 1) * SSM_STATE]
        cm = act[:, SSM_WIDTH + (SSM_GROUPS + g) * SSM_STATE:SSM_WIDTH + (SSM_GROUPS + g + 1) * SSM_STATE]
        bm_t = bm.T
        cb = _dot(cm, bm_t)
        gs = slice(g * gw, (g + 1) * gw)
        st = st_ref[g]
        y_ref[:, gs] = _dot(cm, st) * e_cum[:, gs]
        st_ref[g] = st * e_last[:, gs] + _dot(bm_t, xw[:, gs])
        for hh in range(hg):
            h = g * hg + hh
            seg = cum[:, h:h + 1] - cum_t[h:h + 1, :]
            decay = jnp.exp(jnp.where(causal, seg, -jnp.inf))
            hs = slice(h * hd, (h + 1) * hd)
            y_ref[:, hs] = y_ref[:, hs] + _dot(cb * decay, xr[:, hs])

    y = (y_ref[...] + dch_ref[...] * xs) * _silu(z_ref[...])
    for g in range(SSM_GROUPS):
        gs = slice(g * gw, (g + 1) * gw)
        yg = y[:, gs]
        yg = yg * lax.rsqrt(jnp.mean(yg * yg, axis=-1, keepdims=True) + EPS)
        o_ref[:, gs] = (yg * nw_ref[:, gs]).astype(o_ref.dtype)


def _ssd(proj, conv_w, conv_b, dt_bias, a_log, d_skip, norm_w, bsz, seqlen):
    nc = seqlen // SSM_CHUNK
    pad = LANES - SSM_HEADS
    dtb = jnp.pad(dt_bias, (0, pad)).reshape(1, LANES)
    alog = jnp.pad(a_log, (0, pad)).reshape(1, LANES)
    dch = jnp.repeat(d_skip, SSM_HEAD_DIM).reshape(1, SSM_WIDTH)
    expand = (jnp.arange(LANES)[:, None] == (jnp.arange(SSM_WIDTH)[None, :] // SSM_HEAD_DIM)).astype(F32)
    const = lambda shape: pl.BlockSpec(shape, lambda b, c: (0, 0))
    return pl.pallas_call(
        _ssd_kernel,
        grid=(bsz, nc),
        in_specs=[pl.BlockSpec((SSM_CHUNK, SSM_CONV_DIM), lambda b, c: (b * nc + c, COL_XBC // SSM_CONV_DIM)),
                  pl.BlockSpec((SSM_CHUNK, SSM_WIDTH), lambda b, c: (b * nc + c, COL_Z // SSM_WIDTH)),
                  pl.BlockSpec((SSM_CHUNK, LANES), lambda b, c: (b * nc + c, COL_DT // LANES)),
                  const((SSM_CONV, SSM_CONV_DIM)), const((1, SSM_CONV_DIM)),
                  const((1, LANES)), const((1, LANES)), const((1, SSM_WIDTH)), const((1, SSM_WIDTH)),
                  const((LANES, SSM_WIDTH))],
        out_specs=pl.BlockSpec((SSM_CHUNK, SSM_WIDTH), lambda b, c: (b * nc + c, 0)),
        out_shape=jax.ShapeDtypeStruct((bsz * seqlen, SSM_WIDTH), BF16),
        scratch_shapes=[pltpu.VMEM((SSM_CHUNK + 8, SSM_CONV_DIM), F32),
                        pltpu.VMEM((SSM_GROUPS, SSM_STATE, SSM_WIDTH // SSM_GROUPS), F32),
                        pltpu.VMEM((SSM_CHUNK, SSM_WIDTH), F32)],
        compiler_params=_params("parallel", "arbitrary"),
        name="ssd",
    )(proj, proj, proj, conv_w, conv_b.reshape(1, -1), dtb, alog, dch, norm_w.reshape(1, -1), expand)


CF_TILE = 256
CF_HALO = 32
CF_ROWS = 64


def _conformer_kernel(u_ref, cw_ref, cb_ref, lw_ref, lb_ref, o_ref, buf_ref, cv_ref):
    i = pl.program_id(1)
    ts = CF_TILE

    @pl.when(i == 0)
    def _():
        buf_ref[0:CF_HALO, :] = jnp.zeros((CF_HALO, CONV_WIDTH), F32)

    @pl.when(i > 0)
    def _():
        buf_ref[0:CF_HALO, :] = buf_ref[ts:ts + CF_HALO, :]

    buf_ref[CF_HALO:CF_HALO + ts, :] = u_ref[:, :CONV_WIDTH] * jax.nn.sigmoid(u_ref[:, CONV_WIDTH:])
    base = CF_HALO - (CONV_SIZE - 1)
    for r in range(ts // CF_ROWS):
        for cc in range(CONV_WIDTH // LANES):
            cs = slice(cc * LANES, (cc + 1) * LANES)
            acc = jnp.zeros((CF_ROWS, LANES), F32) + cb_ref[:, cs]
            for k in range(CONV_SIZE):
                off = r * CF_ROWS + base + k
                acc = acc + buf_ref[off:off + CF_ROWS, cs] * cw_ref[k:k + 1, cs]
            cv_ref[r * CF_ROWS:(r + 1) * CF_ROWS, cs] = acc
    hf = cv_ref[...]
    mu = jnp.mean(hf, axis=-1, keepdims=True)
    d = hf - mu
    var = jnp.mean(d * d, axis=-1, keepdims=True)
    o_ref[...] = _silu(d * lax.rsqrt(var + EPS) * lw_ref[...] + lb_ref[...]).astype(o_ref.dtype)


def _conformer(proj, conv_w, conv_b, ln_w, ln_b, bsz, seqlen):
    nt = seqlen // CF_TILE
    const = lambda shape: pl.BlockSpec(shape, lambda b, i: (0, 0))
    return pl.pallas_call(
        _conformer_kernel,
        grid=(bsz, nt),
        in_specs=[pl.BlockSpec((CF_TILE, 2 * CONV_WIDTH), lambda b, i: (b * nt + i, COL_GLU // (2 * CONV_WIDTH))),
                  const((CONV_SIZE, CONV_WIDTH)), const((1, CONV_WIDTH)),
                  const((1, CONV_WIDTH)), const((1, CONV_WIDTH))],
        out_specs=pl.BlockSpec((CF_TILE, CONV_WIDTH), lambda b, i: (b * nt + i, 0)),
        out_shape=jax.ShapeDtypeStruct((bsz * seqlen, CONV_WIDTH), BF16),
        scratch_shapes=[pltpu.VMEM((CF_HALO + CF_TILE, CONV_WIDTH), F32),
                        pltpu.VMEM((CF_TILE, CONV_WIDTH), F32)],
        compiler_params=_params("parallel", "arbitrary"),
        name="conformer",
    )(proj, conv_w, conv_b.reshape(1, -1), ln_w.reshape(1, -1), ln_b.reshape(1, -1))


def _outproj_kernel(a_ref, s_ref, c_ref, wa_ref, ws_ref, wc_ref, h_ref, o_ref):
    o_ref[...] = (h_ref[...] + _dot(a_ref[...], wa_ref[...]) + _dot(s_ref[...], ws_ref[...])
                  + _dot(c_ref[...], wc_ref[...]))


def _outproj(attn, ssm, conv, w_out, h, tm=512):
    t = h.shape[0]
    row = lambda w: pl.BlockSpec((tm, w), lambda i: (i, 0))
    return pl.pallas_call(
        _outproj_kernel,
        grid=(t // tm,),
        in_specs=[row(ATTN_WIDTH), row(SSM_WIDTH), row(CONV_WIDTH),
                  pl.BlockSpec((ATTN_WIDTH, D_MODEL), lambda i: (0, 0)),
                  pl.BlockSpec((SSM_WIDTH, D_MODEL), lambda i: (ATTN_WIDTH // SSM_WIDTH, 0)),
                  pl.BlockSpec((CONV_WIDTH, D_MODEL), lambda i: ((ATTN_WIDTH + SSM_WIDTH) // CONV_WIDTH, 0)),
                  row(D_MODEL)],
        out_specs=row(D_MODEL),
        out_shape=jax.ShapeDtypeStruct((t, D_MODEL), F32),
        compiler_params=_params("parallel"),
        name="outproj",
    )(attn, ssm, conv, w_out, w_out, w_out, h)


def _ffn_kernel(h_ref, nw_ref, wg_ref, wu_ref, wd_ref, o_ref, hn_ref):
    @pl.when(pl.program_id(1) == 0)
    def _():
        h = h_ref[...]
        hn_ref[...] = _rms(h, nw_ref[...]).astype(BF16)
        o_ref[...] = h

    hn = hn_ref[...]
    a = _silu(_dot(hn, wg_ref[...])) * _dot(hn, wu_ref[...])
    o_ref[...] += _dot(a.astype(BF16), wd_ref[...])


def _ffn(h, nw, wg, wu, wd, tm=1024, tf=512):
    t = h.shape[0]
    return pl.pallas_call(
        _ffn_kernel,
        grid=(t // tm, D_FF // tf),
        in_specs=[pl.BlockSpec((tm, D_MODEL), lambda i, f: (i, 0)),
                  pl.BlockSpec((1, D_MODEL), lambda i, f: (0, 0)),
                  pl.BlockSpec((D_MODEL, tf), lambda i, f: (0, f)),
                  pl.BlockSpec((D_MODEL, tf), lambda i, f: (0, f)),
                  pl.BlockSpec((tf, D_MODEL), lambda i, f: (f, 0))],
        out_specs=pl.BlockSpec((tm, D_MODEL), lambda i, f: (i, 0)),
        out_shape=jax.ShapeDtypeStruct((t, D_MODEL), F32),
        scratch_shapes=[pltpu.VMEM((tm, D_MODEL), BF16)],
        compiler_params=_params("parallel", "arbitrary"),
        name="ffn",
    )(h, nw, wg, wu, wd)


TOK_CHUNK = 256
ROW_BLOCK = 256
MOE_TM = 768


def _router_kernel(h_ref, nw_ref, wr_ref, hn_ref, comb_ref, rank_ref, rankt_ref, cnt_ref, carry_ref):
    @pl.when(pl.program_id(0) == 0)
    def _():
        carry_ref[...] = jnp.zeros_like(carry_ref)

    hn = _rms(h_ref[...], nw_ref[...])
    hn_ref[...] = hn.astype(BF16)
    logits = _dot_hi(hn, wr_ref[...])
    lane = lax.broadcasted_iota(jnp.int32, logits.shape, 1)
    logits = jnp.where(lane < N_EXPERTS, logits, -jnp.inf)
    m1 = jnp.max(logits, axis=1, keepdims=True)
    i1 = jnp.min(jnp.where(logits == m1, lane, LANES), axis=1, keepdims=True)
    rest = jnp.where(lane == i1, -jnp.inf, logits)
    m2 = jnp.max(rest, axis=1, keepdims=True)
    i2 = jnp.min(jnp.where(rest == m2, lane, LANES), axis=1, keepdims=True)
    e2 = jnp.exp(m2 - m1)
    den = 1.0 + e2
    comb_ref[...] = jnp.where(lane == i1, 1.0 / den, 0.0) + jnp.where(lane == i2, e2 / den, 0.0)
    sel = (lane == i1) | (lane == i2)
    self = jnp.where(sel, 1.0, 0.0)
    n = logits.shape[0]
    tri = jnp.where(lax.broadcasted_iota(jnp.int32, (n, n), 1) <= lax.broadcasted_iota(jnp.int32, (n, n), 0),
                    1.0, 0.0).astype(BF16)
    incl = _dot(tri, self.astype(BF16))
    carry = carry_ref[0:1, :]
    rank = jnp.where(sel, carry + incl - self, -1.0)
    rank_ref[...] = rank
    rankt_ref[...] = rank.T[:N_EXPERTS, :]
    carry_ref[...] = jnp.broadcast_to(carry + incl[n - 1:n, :], carry_ref.shape)
    cnt_ref[...] = carry_ref[...]


def _router(h, nw, wr):
    t = h.shape[0]
    nch = t // TOK_CHUNK
    wr_p = jnp.pad(wr, ((0, 0), (0, LANES - N_EXPERTS)))
    return pl.pallas_call(
        _router_kernel,
        grid=(nch,),
        in_specs=[pl.BlockSpec((TOK_CHUNK, D_MODEL), lambda i: (i, 0)),
                  pl.BlockSpec((1, D_MODEL), lambda i: (0, 0)),
                  pl.BlockSpec((D_MODEL, LANES), lambda i: (0, 0))],
        out_specs=[pl.BlockSpec((TOK_CHUNK, D_MODEL), lambda i: (i, 0)),
                   pl.BlockSpec((TOK_CHUNK, LANES), lambda i: (i, 0)),
                   pl.BlockSpec((TOK_CHUNK, LANES), lambda i: (i, 0)),
                   pl.BlockSpec((N_EXPERTS, TOK_CHUNK), lambda i: (0, i)),
                   pl.BlockSpec((None, 8, LANES), lambda i: (i, 0, 0))],
        out_shape=[jax.ShapeDtypeStruct((t, D_MODEL), BF16),
                   jax.ShapeDtypeStruct((t, LANES), F32),
                   jax.ShapeDtypeStruct((t, LANES), F32),
                   jax.ShapeDtypeStruct((N_EXPERTS, t), F32),
                   jax.ShapeDtypeStruct((nch, 8, LANES), F32)],
        scratch_shapes=[pltpu.VMEM((8, LANES), F32)],
        compiler_params=_params("arbitrary"),
        name="router",
    )(h, nw, wr_p)


def _moe_plan(cnt_after, t):
    nch = t // TOK_CHUNK
    n_tiles = (t * 2) // MOE_TM + N_EXPERTS
    sub = MOE_TM // ROW_BLOCK
    cnta = cnt_after[:, 0, :N_EXPERTS].astype(jnp.int32)
    cntb = jnp.concatenate([jnp.zeros((1, N_EXPERTS), jnp.int32), cnta[:-1]], axis=0)
    n = cnta[-1]
    tiles_e = (n + MOE_TM - 1) // MOE_TM
    tile_end = jnp.cumsum(tiles_e)
    tile_start = tile_end - tiles_e
    n_valid = tile_end[-1]
    eid = jnp.arange(N_EXPERTS, dtype=jnp.int32)

    def count(cond, axis):
        return jnp.sum(cond.astype(jnp.int32), axis=axis)

    def take(table, idx):
        hot = (idx[:, None] == eid[None, :]).astype(jnp.int32)
        if table.ndim == 1:
            return jnp.sum(hot * table[None, :], axis=1)
        return jnp.sum(hot[:, :, None] * table[None, :, :], axis=1)

    m = jnp.arange(n_tiles, dtype=jnp.int32)
    last_e = count(tile_end <= n_valid - 1, 0)
    tile_e = jnp.where(m < n_valid, count(tile_end[None, :] <= m[:, None], 1), last_e)
    tile_e = jnp.minimum(tile_e, N_EXPERTS - 1).astype(jnp.int32)
    rows_left = take(n, tile_e) - (m - take(tile_start, tile_e)) * MOE_TM
    tile_nsb = jnp.where(m < n_valid, jnp.clip((rows_left + ROW_BLOCK - 1) // ROW_BLOCK, 0, sub), 0).astype(jnp.int32)
    seg_row = (tile_start * MOE_TM).astype(jnp.int32)
    sb = jnp.arange(n_tiles * sub, dtype=jnp.int32)
    sb_e = jnp.repeat(tile_e, sub)
    sb_n = take(n, sb_e)
    sb_r0 = sb * ROW_BLOCK - take(seg_row, sb_e)
    sb_valid = ((sb // sub) < n_valid) & (sb_r0 < sb_n)
    r1 = jnp.minimum(sb_r0 + ROW_BLOCK, sb_n)
    lo = count(take(cnta.T, sb_e) <= sb_r0[:, None], 1)
    hi = count(take(cntb.T, sb_e) < r1[:, None], 1) - 1
    sb_lo = jnp.where(sb_valid, lo, 0)
    sb_hi = jnp.where(sb_valid, hi, -1)
    comb_b0 = ((seg_row[None, :] + cntb) // ROW_BLOCK).astype(jnp.int32).reshape(-1)
    return dict(tile_e=tile_e, tile_nsb=tile_nsb, seg_row=seg_row, sb_e=sb_e, sb_r0=sb_r0.astype(jnp.int32),
                sb_lo=sb_lo, sb_hi=sb_hi, comb_b0=comb_b0, n_tiles=n_tiles)


def _dispatch_kernel(e_ref, r0_ref, lo_ref, hi_ref, hn_ref, comb_ref, rankt_ref, xs_ref, gs_ref, accx_ref, accg_ref,
                     *, nch):
    sb = pl.program_id(0)
    e = e_ref[sb]
    rowid = (lax.broadcasted_iota(jnp.int32, (ROW_BLOCK, TOK_CHUNK), 0) + r0_ref[sb]).astype(F32)
    accx_ref[...] = jnp.zeros_like(accx_ref)
    accg_ref[...] = jnp.zeros_like(accg_ref)

    def body(c, _):
        c0 = pl.multiple_of(c * TOK_CHUNK, TOK_CHUNK)
        p = rowid == rankt_ref[pl.ds(e * nch + c, 1), :]
        accx_ref[...] += _dot(jnp.where(p, 1.0, 0.0).astype(BF16), hn_ref[pl.ds(c0, TOK_CHUNK), :])
        accg_ref[...] += _dot_hi(jnp.where(p, 1.0, 0.0), comb_ref[pl.ds(c0, TOK_CHUNK), :])
        return 0

    lax.fori_loop(lo_ref[sb], hi_ref[sb] + 1, body, 0)
    xs_ref[...] = accx_ref[...].astype(BF16)
    lane = lax.broadcasted_iota(jnp.int32, accg_ref.shape, 1)
    g = jnp.sum(jnp.where(lane == e, accg_ref[...], 0.0), axis=1, keepdims=True)
    gs_ref[...] = jnp.broadcast_to(g, gs_ref.shape)


def _dispatch(plan, hn, comb, rankt):
    t = hn.shape[0]
    nch = t // TOK_CHUNK
    nsb = plan['n_tiles'] * (MOE_TM // ROW_BLOCK)
    whole = pl.BlockSpec(memory_space=pltpu.VMEM)
    return pl.pallas_call(
        functools.partial(_dispatch_kernel, nch=nch),
        grid_spec=pltpu.PrefetchScalarGridSpec(
            num_scalar_prefetch=4,
            grid=(nsb,),
            in_specs=[whole, whole, whole],
            out_specs=[pl.BlockSpec((ROW_BLOCK, D_MODEL), lambda s, *_: (s, 0)),
                       pl.BlockSpec((ROW_BLOCK, LANES), lambda s, *_: (s, 0))],
            scratch_shapes=[pltpu.VMEM((ROW_BLOCK, D_MODEL), F32), pltpu.VMEM((ROW_BLOCK, LANES), F32)]),
        out_shape=[jax.ShapeDtypeStruct((nsb * ROW_BLOCK, D_MODEL), BF16),
                   jax.ShapeDtypeStruct((nsb * ROW_BLOCK, LANES), F32)],
        compiler_params=_params("arbitrary"),
        name="dispatch",
    )(plan['sb_e'], plan['sb_r0'], plan['sb_lo'], plan['sb_hi'], hn, comb, rankt.reshape(N_EXPERTS * nch, TOK_CHUNK))


def _moe_kernel(te_ref, nsb_ref, x_ref, gs_ref, wg_ref, wu_ref, wd_ref, y_ref, acc_ref, wgb_ref, wub_ref, wdb_ref):
    m = pl.program_id(0)
    f = pl.program_id(1)
    nsb = nsb_ref[m]
    sub = MOE_TM // ROW_BLOCK

    @pl.when(f == 0)
    def _():
        acc_ref[...] = jnp.zeros_like(acc_ref)

    @pl.when(nsb > 0)
    def _():
        wgb_ref[...] = wg_ref[...].astype(BF16)
        wub_ref[...] = wu_ref[...].astype(BF16)
        wdb_ref[...] = wd_ref[...].astype(BF16)

    def block(rows):
        x = x_ref[rows, :]
        a = _silu(_dot(x, wgb_ref[...])) * _dot(x, wub_ref[...])
        acc_ref[rows, :] += _dot(a.astype(BF16), wdb_ref[...])

    @pl.when(nsb == sub)
    def _():
        block(slice(None))

    @pl.when((nsb > 0) & (nsb < sub))
    def _():
        def body(s, _):
            block(pl.ds(pl.multiple_of(s * ROW_BLOCK, ROW_BLOCK), ROW_BLOCK))
            return 0
        lax.fori_loop(0, nsb, body, 0)

    @pl.when(f == pl.num_programs(1) - 1)
    def _():
        y_ref[...] = (acc_ref[...] * gs_ref[:, 0:1]).astype(y_ref.dtype)


def _moe(plan, xs, gs, wg, wu, wd, tf=512):
    n_tiles = plan['n_tiles']
    nf = D_FF // tf

    def fsel(m, f, nsb):
        return jnp.where(nsb[m] > 0, f, nf - 1)

    return pl.pallas_call(
        _moe_kernel,
        grid_spec=pltpu.PrefetchScalarGridSpec(
            num_scalar_prefetch=2,
            grid=(n_tiles, nf),
            in_specs=[pl.BlockSpec((MOE_TM, D_MODEL), lambda m, f, te, nsb: (m, 0)),
                      pl.BlockSpec((MOE_TM, LANES), lambda m, f, te, nsb: (m, 0)),
                      pl.BlockSpec((None, D_MODEL, tf), lambda m, f, te, nsb: (te[m], 0, fsel(m, f, nsb))),
                      pl.BlockSpec((None, D_MODEL, tf), lambda m, f, te, nsb: (te[m], 0, fsel(m, f, nsb))),
                      pl.BlockSpec((None, tf, D_MODEL), lambda m, f, te, nsb: (te[m], fsel(m, f, nsb), 0))],
            out_specs=pl.BlockSpec((MOE_TM, D_MODEL), lambda m, f, te, nsb: (m, 0)),
            scratch_shapes=[pltpu.VMEM((MOE_TM, D_MODEL), F32),
                            pltpu.VMEM((D_MODEL, tf), BF16), pltpu.VMEM((D_MODEL, tf), BF16),
                            pltpu.VMEM((tf, D_MODEL), BF16)]),
        out_shape=jax.ShapeDtypeStruct((n_tiles * MOE_TM, D_MODEL), BF16),
        compiler_params=_params("arbitrary", "arbitrary"),
        name="moe",
    )(plan['tile_e'], plan['tile_nsb'], xs, gs, wg, wu, wd)


def _combine_kernel(b0_ref, seg_ref, h_ref, rank_ref, y0_ref, y1_ref, fw_ref, o_ref):
    c = pl.program_id(0)
    e = pl.program_id(1)

    @pl.when(e == 0)
    def _():
        o_ref[...] = h_ref[...]

    rk = rank_ref[...]
    lane = lax.broadcasted_iota(jnp.int32, rk.shape, 1)
    r = jnp.sum(jnp.where(lane == e, rk, 0.0), axis=1, keepdims=True)
    shift = (seg_ref[e] - b0_ref[c * N_EXPERTS + e] * ROW_BLOCK).astype(F32)
    loc = jnp.where(r < 0.0, -1.0, r + shift)
    col = lax.broadcasted_iota(jnp.int32, (TOK_CHUNK, ROW_BLOCK), 1).astype(F32)
    q0 = jnp.where(loc == col, 1.0, 0.0).astype(BF16)
    q1 = jnp.where(loc - ROW_BLOCK == col, 1.0, 0.0).astype(BF16)
    o_ref[...] += _dot(q0, y0_ref[...]) + _dot(q1, y1_ref[...])

    @pl.when(e == pl.num_programs(1) - 1)
    def _():
        o_ref[...] = _rms(o_ref[...], fw_ref[...])


def _combine(plan, h, rank, ys, fw):
    t = h.shape[0]
    nch = t // TOK_CHUNK
    nblk = ys.shape[0] // ROW_BLOCK
    return pl.pallas_call(
        _combine_kernel,
        grid_spec=pltpu.PrefetchScalarGridSpec(
            num_scalar_prefetch=2,
            grid=(nch, N_EXPERTS),
            in_specs=[pl.BlockSpec((TOK_CHUNK, D_MODEL), lambda c, e, b0, seg: (c, 0)),
                      pl.BlockSpec((TOK_CHUNK, LANES), lambda c, e, b0, seg: (c, 0)),
                      pl.BlockSpec((ROW_BLOCK, D_MODEL), lambda c, e, b0, seg: (b0[c * N_EXPERTS + e], 0)),
                      pl.BlockSpec((ROW_BLOCK, D_MODEL),
                                   lambda c, e, b0, seg: (jnp.minimum(b0[c * N_EXPERTS + e] + 1, nblk - 1), 0)),
                      pl.BlockSpec((1, D_MODEL), lambda c, e, b0, seg: (0, 0))],
            out_specs=pl.BlockSpec((TOK_CHUNK, D_MODEL), lambda c, e, b0, seg: (c, 0))),
        out_shape=jax.ShapeDtypeStruct((t, D_MODEL), F32),
        compiler_params=_params("arbitrary", "arbitrary"),
        name="combine",
    )(plan['comb_b0'], plan['seg_row'], h, rank, ys, ys, fw)


def _rope_tables(seqlen):
    pos = jnp.arange(seqlen, dtype=F32)
    inv_freq = 1.0 / (ROPE_THETA ** (jnp.arange(0, HEAD_DIM, 2, dtype=F32) / HEAD_DIM))
    ang = pos[:, None] * inv_freq[None, :]
    cos, sin = jnp.cos(ang), jnp.sin(ang)
    return jnp.concatenate([cos, cos], axis=-1), jnp.concatenate([-sin, sin], axis=-1)


def _arrange_w_in(w):
    q_k_v = w[:, :3 * ATTN_WIDTH]
    z = w[:, 3072:3584]
    xbc = w[:, 3584:4608]
    dt = jnp.pad(w[:, 4608:4616], ((0, 0), (0, LANES - SSM_HEADS)))
    glu = w[:, 4616:5640]
    return jnp.concatenate([q_k_v, xbc, glu, z, dt], axis=1).astype(BF16)


def kernel(x, norm_mix, w_in, ssm_conv_w, ssm_conv_b, ssm_dt_bias, ssm_a_log, ssm_d, ssm_norm_w, cf_conv_w, cf_conv_b, cf_ln_w, cf_ln_b, w_out, norm_ffn, ffn_w_gate, ffn_w_up, ffn_w_down, moe_router, moe_w_gate, moe_w_up, moe_w_down, norm_final):
    bsz, seqlen, _ = x.shape
    depth = w_in.shape[0]
    assert depth == 2 and ffn_w_gate.shape[0] == 1 and moe_router.shape[0] == 1
    cosf, sinf = _rope_tables(seqlen)
    h = x.reshape(bsz * seqlen, D_MODEL)
    for layer in range(depth):
        proj = _inproj(h, norm_mix[layer].reshape(1, -1), _arrange_w_in(w_in[layer]))
        attn = _moba(proj, cosf, sinf, bsz, seqlen)
        ssm = _ssd(proj, ssm_conv_w[layer], ssm_conv_b[layer], ssm_dt_bias[layer], ssm_a_log[layer],
                   ssm_d[layer], ssm_norm_w[layer], bsz, seqlen)
        conv = _conformer(proj, cf_conv_w[layer], cf_conv_b[layer], cf_ln_w[layer], cf_ln_b[layer], bsz, seqlen)
        h = _outproj(attn, ssm, conv, w_out[layer].astype(BF16), h)
        nw = norm_ffn[layer].reshape(1, -1)
        if layer % 2 == 0:
            i = layer // 2
            h = _ffn(h, nw, ffn_w_gate[i].astype(BF16), ffn_w_up[i].astype(BF16), ffn_w_down[i].astype(BF16))
        else:
            i = layer // 2
            hn, comb, rank, rankt, cnt_after = _router(h, nw, moe_router[i])
            plan = _moe_plan(cnt_after, h.shape[0])
            xs, gs = _dispatch(plan, hn, comb, rankt)
            ys = _moe(plan, xs, gs, moe_w_gate[i], moe_w_up[i], moe_w_down[i])
            h = _combine(plan, h, rank, ys, norm_final.reshape(1, -1))
    return h.reshape(bsz, seqlen, D_MODEL)
```

```python
import functools
import math

import jax
import jax.numpy as jnp
from jax import lax
from jax.experimental import pallas as pl
from jax.experimental.pallas import tpu as pltpu

F32 = jnp.float32
BF16 = jnp.bfloat16

D_MODEL = 2048
HEAD_DIM = 128
ATTN_WIDTH = 1024
ATTN_HEADS = 8
MOBA_BLOCK = 256
MOBA_TOPK = 3
ROPE_THETA = 10000.0
SSM_WIDTH = 512
SSM_HEAD_DIM = 64
SSM_HEADS = 8
SSM_GROUPS = 2
SSM_STATE = 128
SSM_CONV = 4
SSM_CHUNK = 128
SSM_CONV_DIM = 1024
CONV_WIDTH = 512
CONV_SIZE = 31
D_FF = 7168
N_EXPERTS = 8
EPS = 1e-5

LANES = 128
NEG = -1e30

COL_Q, COL_K, COL_V = 0, 1024, 2048
COL_XBC, COL_GLU, COL_Z, COL_DT = 3072, 4096, 5120, 5632
IN_PAD = 5760

VMEM_LIMIT = 56 * 1024 * 1024


def _params(*sem):
    return pltpu.CompilerParams(dimension_semantics=sem, vmem_limit_bytes=VMEM_LIMIT)


def _dot(a, b):
    return jnp.dot(a, b, preferred_element_type=F32)


def _dot_nt(a, b):
    return lax.dot_general(a, b, (((1,), (1,)), ((), ())), preferred_element_type=F32)


def _dot_hi(a, b):
    return jnp.dot(a, b, preferred_element_type=F32, precision=lax.Precision.HIGHEST)


def _dot_nt_hi(a, b):
    return lax.dot_general(a, b, (((1,), (1,)), ((), ())), preferred_element_type=F32,
                           precision=lax.Precision.HIGHEST)


def _rms(x, w):
    return x * lax.rsqrt(jnp.mean(x * x, axis=-1, keepdims=True) + EPS) * w


def _silu(x):
    return x * jax.nn.sigmoid(x)


def _inproj_kernel(x_ref, nw_ref, w_ref, o_ref, xn_ref):
    @pl.when(pl.program_id(1) == 0)
    def _():
        xn_ref[...] = _rms(x_ref[...], nw_ref[...]).astype(BF16)

    o_ref[...] = _dot(xn_ref[...], w_ref[...])


def _inproj(h, nw, w, tm=1024, tn=1152):
    t = h.shape[0]
    return pl.pallas_call(
        _inproj_kernel,
        grid=(t // tm, IN_PAD // tn),
        in_specs=[pl.BlockSpec((tm, D_MODEL), lambda i, j: (i, 0)),
                  pl.BlockSpec((1, D_MODEL), lambda i, j: (0, 0)),
                  pl.BlockSpec((D_MODEL, tn), lambda i, j: (0, j))],
        out_specs=pl.BlockSpec((tm, tn), lambda i, j: (i, j)),
        out_shape=jax.ShapeDtypeStruct((t, IN_PAD), F32),
        scratch_shapes=[pltpu.VMEM((tm, D_MODEL), BF16)],
        compiler_params=_params("parallel", "arbitrary"),
        name="inproj",
    )(h, nw, w)


def _rope(x, cosf, sinf):
    return x * cosf + pltpu.roll(x, HEAD_DIM // 2, 1) * sinf


MOBA_HP = 2


def _moba_kernel(q_ref, k_ref, v_ref, cos_ref, sin_ref, o_ref, kr_ref, vb_ref, km_ref, *, nb):
    i = pl.program_id(2)
    blk = MOBA_BLOCK
    heads = [slice(hh * HEAD_DIM, (hh + 1) * HEAD_DIM) for hh in range(MOBA_HP)]

    @pl.when(i == 0)
    def _():
        km_ref[...] = jnp.zeros_like(km_ref)
        vb_ref[...] = v_ref[...].astype(BF16)
        for hh, hs in enumerate(heads):
            for j in range(nb):
                sl = slice(j * blk, (j + 1) * blk)
                kr = _rope(k_ref[sl, hs], cos_ref[sl, :], sin_ref[sl, :])
                kr_ref[sl, hs] = kr.astype(BF16)
                km_ref[hh, j:j + 1, :] = jnp.mean(kr, axis=0, keepdims=True)

    row0 = pl.multiple_of(i * blk, blk)
    rows = pl.ds(row0, blk)
    state, init = [], []
    for hh, hs in enumerate(heads):
        q = _rope(q_ref[:, hs], cos_ref[rows, :], sin_ref[rows, :])
        gate = _dot_nt_hi(q, km_ref[hh])
        lane = lax.broadcasted_iota(jnp.int32, gate.shape, 1)
        gate = jnp.where(lane < i, gate, -jnp.inf)
        cnt = jnp.zeros(gate.shape, jnp.int32)
        for c in range(nb - 1):
            gc = gate[:, c:c + 1]
            beats = (gc > gate) | ((gc == gate) & (c < lane))
            cnt = cnt + jnp.where(beats, 1, 0)
        selbias = jnp.where((cnt < MOBA_TOPK) & (lane < i), 0.0, NEG)

        qs = (q * (HEAD_DIM ** -0.5)).astype(BF16)
        s = _dot_nt(qs, kr_ref[rows, hs])
        qpos = lax.broadcasted_iota(jnp.int32, s.shape, 0)
        kpos = lax.broadcasted_iota(jnp.int32, s.shape, 1)
        s = jnp.where(kpos <= qpos, s, NEG)
        m0 = jnp.max(s, axis=1, keepdims=True)
        p = jnp.exp(s - m0)
        l0 = jnp.sum(p, axis=1, keepdims=True)
        acc0 = _dot(p.astype(BF16), vb_ref[rows, hs])
        state.append((hs, lane, selbias, qs))
        init.append((m0, l0, acc0))

    def body(j, carry):
        keys = pl.ds(pl.multiple_of(j * blk, blk), blk)
        out = []
        for (hs, lane, selbias, qs), (m, l, acc) in zip(state, carry):
            bias = jnp.max(jnp.where(lane == j, selbias, NEG), axis=1, keepdims=True)
            sj = _dot_nt(qs, kr_ref[keys, hs]) + bias
            mn = jnp.maximum(m, jnp.max(sj, axis=1, keepdims=True))
            alpha = jnp.exp(m - mn)
            pj = jnp.exp(sj - mn)
            l = alpha * l + jnp.sum(pj, axis=1, keepdims=True)
            acc = alpha * acc + _dot(pj.astype(BF16), vb_ref[keys, hs])
            out.append((mn, l, acc))
        return tuple(out)

    final = lax.fori_loop(0, i, body, tuple(init))
    for (hs, _, _, _), (_, l, acc) in zip(state, final):
        o_ref[:, hs] = (acc / l).astype(o_ref.dtype)


def _moba(proj, cosf, sinf, bsz, seqlen):
    nb = seqlen // MOBA_BLOCK
    hw = MOBA_HP * HEAD_DIM
    return pl.pallas_call(
        functools.partial(_moba_kernel, nb=nb),
        grid=(bsz, ATTN_HEADS // MOBA_HP, nb),
        in_specs=[pl.BlockSpec((MOBA_BLOCK, hw), lambda b, h, i: (b * nb + i, COL_Q // hw + h)),
                  pl.BlockSpec((seqlen, hw), lambda b, h, i: (b, COL_K // hw + h)),
                  pl.BlockSpec((seqlen, hw), lambda b, h, i: (b, COL_V // hw + h)),
                  pl.BlockSpec((seqlen, HEAD_DIM), lambda b, h, i: (0, 0)),
                  pl.BlockSpec((seqlen, HEAD_DIM), lambda b, h, i: (0, 0))],
        out_specs=pl.BlockSpec((MOBA_BLOCK, hw), lambda b, h, i: (b * nb + i, h)),
        out_shape=jax.ShapeDtypeStruct((bsz * seqlen, ATTN_WIDTH), BF16),
        scratch_shapes=[pltpu.VMEM((seqlen, hw), BF16),
                        pltpu.VMEM((seqlen, hw), BF16),
                        pltpu.VMEM((MOBA_HP, LANES, HEAD_DIM), F32)],
        compiler_params=_params("parallel", "parallel", "arbitrary"),
        name="moba",
    )(proj, proj, proj, cosf, sinf)


def _ssd_kernel(xbc_ref, z_ref, dt_ref, cw_ref, cb_ref, dtb_ref, alog_ref, dch_ref, nw_ref, ex_ref,
                o_ref, xpad_ref, st_ref, y_ref):
    c = pl.program_id(1)
    cl = SSM_CHUNK
    hd = SSM_HEAD_DIM
    gw = SSM_WIDTH // SSM_GROUPS
    hg = SSM_HEADS // SSM_GROUPS

    @pl.when(c == 0)
    def _():
        xpad_ref[0:8, :] = jnp.zeros((8, SSM_CONV_DIM), F32)
        st_ref[...] = jnp.zeros_like(st_ref)

    @pl.when(c > 0)
    def _():
        xpad_ref[0:8, :] = xpad_ref[cl:cl + 8, :]

    xpad_ref[8:8 + cl, :] = xbc_ref[...]
    conv = jnp.zeros((cl, SSM_CONV_DIM), F32) + cb_ref[...]
    for k in range(SSM_CONV):
        off = 8 - (SSM_CONV - 1) + k
        conv = conv + xpad_ref[off:off + cl, :] * cw_ref[k:k + 1, :]
    act = _silu(conv)
    xs = act[:, :SSM_WIDTH]

    dt = jax.nn.softplus(dt_ref[...] + dtb_ref[...])
    la = dt * (-jnp.exp(alog_ref[...]))
    ti = lax.broadcasted_iota(jnp.int32, (cl, cl), 0)
    si = lax.broadcasted_iota(jnp.int32, (cl, cl), 1)
    causal = si <= ti
    cum = _dot_hi(jnp.where(causal, 1.0, 0.0), la)
    cum_t = cum.T
    ex = ex_ref[...]
    dt_c = _dot_hi(dt, ex)
    cum_c = _dot_hi(cum, ex)
    cum_last = cum_c[cl - 1:cl, :]
    xr = xs * dt_c
    xw = xr * jnp.exp(cum_last - cum_c)
    e_cum = jnp.exp(cum_c)
    e_last = jnp.exp(cum_last)

    for g in range(SSM_GROUPS):
        bm = act[:, SSM_WIDTH + g * SSM_STATE:SSM_WIDTH + (g + 1) * SSM_STATE]
        cm = act[:, SSM_WIDTH + (SSM_GROUPS + g) * SSM_STATE:SSM_WIDTH + (SSM_GROUPS + g + 1) * SSM_STATE]
        bm_t = bm.T
        cb = _dot(cm, bm_t)
        gs = slice(g * gw, (g + 1) * gw)
        st = st_ref[g]
        y_ref[:, gs] = _dot(cm, st) * e_cum[:, gs]
        st_ref[g] = st * e_last[:, gs] + _dot(bm_t, xw[:, gs])
        for hh in range(hg):
            h = g * hg + hh
            seg = cum[:, h:h + 1] - cum_t[h:h + 1, :]
            decay = jnp.exp(jnp.where(causal, seg, -jnp.inf))
            hs = slice(h * hd, (h + 1) * hd)
            y_ref[:, hs] = y_ref[:, hs] + _dot(cb * decay, xr[:, hs])

    y = (y_ref[...] + dch_ref[...] * xs) * _silu(z_ref[...])
    for g in range(SSM_GROUPS):
        gs = slice(g * gw, (g + 1) * gw)
        yg = y[:, gs]
        yg = yg * lax.rsqrt(jnp.mean(yg * yg, axis=-1, keepdims=True) + EPS)
        o_ref[:, gs] = (yg * nw_ref[:, gs]).astype(o_ref.dtype)


def _ssd(proj, conv_w, conv_b, dt_bias, a_log, d_skip, norm_w, bsz, seqlen):
    nc = seqlen // SSM_CHUNK
    pad = LANES - SSM_HEADS
    dtb = jnp.pad(dt_bias, (0, pad)).reshape(1, LANES)
    alog = jnp.pad(a_log, (0, pad)).reshape(1, LANES)
    dch = jnp.repeat(d_skip, SSM_HEAD_DIM).reshape(1, SSM_WIDTH)
    expand = (jnp.arange(LANES)[:, None] == (jnp.arange(SSM_WIDTH)[None, :] // SSM_HEAD_DIM)).astype(F32)
    const = lambda shape: pl.BlockSpec(shape, lambda b, c: (0, 0))
    return pl.pallas_call(
        _ssd_kernel,
        grid=(bsz, nc),
        in_specs=[pl.BlockSpec((SSM_CHUNK, SSM_CONV_DIM), lambda b, c: (b * nc + c, COL_XBC // SSM_CONV_DIM)),
                  pl.BlockSpec((SSM_CHUNK, SSM_WIDTH), lambda b, c: (b * nc + c, COL_Z // SSM_WIDTH)),
                  pl.BlockSpec((SSM_CHUNK, LANES), lambda b, c: (b * nc + c, COL_DT // LANES)),
                  const((SSM_CONV, SSM_CONV_DIM)), const((1, SSM_CONV_DIM)),
                  const((1, LANES)), const((1, LANES)), const((1, SSM_WIDTH)), const((1, SSM_WIDTH)),
                  const((LANES, SSM_WIDTH))],
        out_specs=pl.BlockSpec((SSM_CHUNK, SSM_WIDTH), lambda b, c: (b * nc + c, 0)),
        out_shape=jax.ShapeDtypeStruct((bsz * seqlen, SSM_WIDTH), BF16),
        scratch_shapes=[pltpu.VMEM((SSM_CHUNK + 8, SSM_CONV_DIM), F32),
                        pltpu.VMEM((SSM_GROUPS, SSM_STATE, SSM_WIDTH // SSM_GROUPS), F32),
                        pltpu.VMEM((SSM_CHUNK, SSM_WIDTH), F32)],
        compiler_params=_params("parallel", "arbitrary"),
        name="ssd",
    )(proj, proj, proj, conv_w, conv_b.reshape(1, -1), dtb, alog, dch, norm_w.reshape(1, -1), expand)


CF_TILE = 256
CF_HALO = 32
CF_ROWS = 64


def _conformer_kernel(u_ref, cw_ref, cb_ref, lw_ref, lb_ref, o_ref, buf_ref, cv_ref):
    i = pl.program_id(1)
    ts = CF_TILE

    @pl.when(i == 0)
    def _():
        buf_ref[0:CF_HALO, :] = jnp.zeros((CF_HALO, CONV_WIDTH), F32)

    @pl.when(i > 0)
    def _():
        buf_ref[0:CF_HALO, :] = buf_ref[ts:ts + CF_HALO, :]

    buf_ref[CF_HALO:CF_HALO + ts, :] = u_ref[:, :CONV_WIDTH] * jax.nn.sigmoid(u_ref[:, CONV_WIDTH:])
    base = CF_HALO - (CONV_SIZE - 1)
    for r in range(ts // CF_ROWS):
        for cc in range(CONV_WIDTH // LANES):
            cs = slice(cc * LANES, (cc + 1) * LANES)
            acc = jnp.zeros((CF_ROWS, LANES), F32) + cb_ref[:, cs]
            for k in range(CONV_SIZE):
                off = r * CF_ROWS + base + k
                acc = acc + buf_ref[off:off + CF_ROWS, cs] * cw_ref[k:k + 1, cs]
            cv_ref[r * CF_ROWS:(r + 1) * CF_ROWS, cs] = acc
    hf = cv_ref[...]
    mu = jnp.mean(hf, axis=-1, keepdims=True)
    d = hf - mu
    var = jnp.mean(d * d, axis=-1, keepdims=True)
    o_ref[...] = _silu(d * lax.rsqrt(var + EPS) * lw_ref[...] + lb_ref[...]).astype(o_ref.dtype)


def _conformer(proj, conv_w, conv_b, ln_w, ln_b, bsz, seqlen):
    nt = seqlen // CF_TILE
    const = lambda shape: pl.BlockSpec(shape, lambda b, i: (0, 0))
    return pl.pallas_call(
        _conformer_kernel,
        grid=(bsz, nt),
        in_specs=[pl.BlockSpec((CF_TILE, 2 * CONV_WIDTH), lambda b, i: (b * nt + i, COL_GLU // (2 * CONV_WIDTH))),
                  const((CONV_SIZE, CONV_WIDTH)), const((1, CONV_WIDTH)),
                  const((1, CONV_WIDTH)), const((1, CONV_WIDTH))],
        out_specs=pl.BlockSpec((CF_TILE, CONV_WIDTH), lambda b, i: (b * nt + i, 0)),
        out_shape=jax.ShapeDtypeStruct((bsz * seqlen, CONV_WIDTH), BF16),
        scratch_shapes=[pltpu.VMEM((CF_HALO + CF_TILE, CONV_WIDTH), F32),
                        pltpu.VMEM((CF_TILE, CONV_WIDTH), F32)],
        compiler_params=_params("parallel", "arbitrary"),
        name="conformer",
    )(proj, conv_w, conv_b.reshape(1, -1), ln_w.reshape(1, -1), ln_b.reshape(1, -1))


def _outproj_kernel(a_ref, s_ref, c_ref, wa_ref, ws_ref, wc_ref, h_ref, o_ref):
    o_ref[...] = (h_ref[...] + _dot(a_ref[...], wa_ref[...]) + _dot(s_ref[...], ws_ref[...])
                  + _dot(c_ref[...], wc_ref[...]))


def _outproj(attn, ssm, conv, w_out, h, tm=512):
    t = h.shape[0]
    row = lambda w: pl.BlockSpec((tm, w), lambda i: (i, 0))
    return pl.pallas_call(
        _outproj_kernel,
        grid=(t // tm,),
        in_specs=[row(ATTN_WIDTH), row(SSM_WIDTH), row(CONV_WIDTH),
                  pl.BlockSpec((ATTN_WIDTH, D_MODEL), lambda i: (0, 0)),
                  pl.BlockSpec((SSM_WIDTH, D_MODEL), lambda i: (ATTN_WIDTH // SSM_WIDTH, 0)),
                  pl.BlockSpec((CONV_WIDTH, D_MODEL), lambda i: ((ATTN_WIDTH + SSM_WIDTH) // CONV_WIDTH, 0)),
                  row(D_MODEL)],
        out_specs=row(D_MODEL),
        out_shape=jax.ShapeDtypeStruct((t, D_MODEL), F32),
        compiler_params=_params("parallel"),
        name="outproj",
    )(attn, ssm, conv, w_out, w_out, w_out, h)


def _ffn_kernel(h_ref, nw_ref, wg_ref, wu_ref, wd_ref, o_ref, hn_ref):
    @pl.when(pl.program_id(1) == 0)
    def _():
        h = h_ref[...]
        hn_ref[...] = _rms(h, nw_ref[...]).astype(BF16)
        o_ref[...] = h

    hn = hn_ref[...]
    a = _silu(_dot(hn, wg_ref[...])) * _dot(hn, wu_ref[...])
    o_ref[...] += _dot(a.astype(BF16), wd_ref[...])


def _ffn(h, nw, wg, wu, wd, tm=1024, tf=512):
    t = h.shape[0]
    return pl.pallas_call(
        _ffn_kernel,
        grid=(t // tm, D_FF // tf),
        in_specs=[pl.BlockSpec((tm, D_MODEL), lambda i, f: (i, 0)),
                  pl.BlockSpec((1, D_MODEL), lambda i, f: (0, 0)),
                  pl.BlockSpec((D_MODEL, tf), lambda i, f: (0, f)),
                  pl.BlockSpec((D_MODEL, tf), lambda i, f: (0, f)),
                  pl.BlockSpec((tf, D_MODEL), lambda i, f: (f, 0))],
        out_specs=pl.BlockSpec((tm, D_MODEL), lambda i, f: (i, 0)),
        out_shape=jax.ShapeDtypeStruct((t, D_MODEL), F32),
        scratch_shapes=[pltpu.VMEM((tm, D_MODEL), BF16)],
        compiler_params=_params("parallel", "arbitrary"),
        name="ffn",
    )(h, nw, wg, wu, wd)


TOK_CHUNK = 256
ROW_BLOCK = 256
MOE_TM = 768


def _router_kernel(h_ref, nw_ref, wr_ref, hn_ref, comb_ref, rank_ref, rankt_ref, cnt_ref, carry_ref):
    @pl.when(pl.program_id(0) == 0)
    def _():
        carry_ref[...] = jnp.zeros_like(carry_ref)

    hn = _rms(h_ref[...], nw_ref[...])
    hn_ref[...] = hn.astype(BF16)
    logits = _dot_hi(hn, wr_ref[...])
    lane = lax.broadcasted_iota(jnp.int32, logits.shape, 1)
    logits = jnp.where(lane < N_EXPERTS, logits, -jnp.inf)
    m1 = jnp.max(logits, axis=1, keepdims=True)
    i1 = jnp.min(jnp.where(logits == m1, lane, LANES), axis=1, keepdims=True)
    rest = jnp.where(lane == i1, -jnp.inf, logits)
    m2 = jnp.max(rest, axis=1, keepdims=True)
    i2 = jnp.min(jnp.where(rest == m2, lane, LANES), axis=1, keepdims=True)
    e2 = jnp.exp(m2 - m1)
    den = 1.0 + e2
    comb_ref[...] = jnp.where(lane == i1, 1.0 / den, 0.0) + jnp.where(lane == i2, e2 / den, 0.0)
    sel = (lane == i1) | (lane == i2)
    self = jnp.where(sel, 1.0, 0.0)
    n = logits.shape[0]
    tri = jnp.where(lax.broadcasted_iota(jnp.int32, (n, n), 1) <= lax.broadcasted_iota(jnp.int32, (n, n), 0),
                    1.0, 0.0).astype(BF16)
    incl = _dot(tri, self.astype(BF16))
    carry = carry_ref[0:1, :]
    rank = jnp.where(sel, carry + incl - self, -1.0)
    rank_ref[...] = rank
    rankt_ref[...] = rank.T[:N_EXPERTS, :]
    carry_ref[...] = jnp.broadcast_to(carry + incl[n - 1:n, :], carry_ref.shape)
    cnt_ref[...] = carry_ref[...]


def _router(h, nw, wr):
    t = h.shape[0]
    nch = t // TOK_CHUNK
    wr_p = jnp.pad(wr, ((0, 0), (0, LANES - N_EXPERTS)))
    return pl.pallas_call(
        _router_kernel,
        grid=(nch,),
        in_specs=[pl.BlockSpec((TOK_CHUNK, D_MODEL), lambda i: (i, 0)),
                  pl.BlockSpec((1, D_MODEL), lambda i: (0, 0)),
                  pl.BlockSpec((D_MODEL, LANES), lambda i: (0, 0))],
        out_specs=[pl.BlockSpec((TOK_CHUNK, D_MODEL), lambda i: (i, 0)),
                   pl.BlockSpec((TOK_CHUNK, LANES), lambda i: (i, 0)),
                   pl.BlockSpec((TOK_CHUNK, LANES), lambda i: (i, 0)),
                   pl.BlockSpec((N_EXPERTS, TOK_CHUNK), lambda i: (0, i)),
                   pl.BlockSpec((None, 8, LANES), lambda i: (i, 0, 0))],
        out_shape=[jax.ShapeDtypeStruct((t, D_MODEL), BF16),
                   jax.ShapeDtypeStruct((t, LANES), F32),
                   jax.ShapeDtypeStruct((t, LANES), F32),
                   jax.ShapeDtypeStruct((N_EXPERTS, t), F32),
                   jax.ShapeDtypeStruct((nch, 8, LANES), F32)],
        scratch_shapes=[pltpu.VMEM((8, LANES), F32)],
        compiler_params=_params("arbitrary"),
        name="router",
    )(h, nw, wr_p)


def _moe_plan(cnt_after, t):
    nch = t // TOK_CHUNK
    n_tiles = (t * 2) // MOE_TM + N_EXPERTS
    sub = MOE_TM // ROW_BLOCK
    cnta = cnt_after[:, 0, :N_EXPERTS].astype(jnp.int32)
    cntb = jnp.concatenate([jnp.zeros((1, N_EXPERTS), jnp.int32), cnta[:-1]], axis=0)
    n = cnta[-1]
    tiles_e = (n + MOE_TM - 1) // MOE_TM
    tile_end = jnp.cumsum(tiles_e)
    tile_start = tile_end - tiles_e
    n_valid = tile_end[-1]
    eid = jnp.arange(N_EXPERTS, dtype=jnp.int32)

    def count(cond, axis):
        return jnp.sum(cond.astype(jnp.int32), axis=axis)

    def take(table, idx):
        hot = (idx[:, None] == eid[None, :]).astype(jnp.int32)
        if table.ndim == 1:
            return jnp.sum(hot * table[None, :], axis=1)
        return jnp.sum(hot[:, :, None] * table[None, :, :], axis=1)

    m = jnp.arange(n_tiles, dtype=jnp.int32)
    last_e = count(tile_end <= n_valid - 1, 0)
    tile_e = jnp.where(m < n_valid, count(tile_end[None, :] <= m[:, None], 1), last_e)
    tile_e = jnp.minimum(tile_e, N_EXPERTS - 1).astype(jnp.int32)
    rows_left = take(n, tile_e) - (m - take(tile_start, tile_e)) * MOE_TM
    tile_nsb = jnp.where(m < n_valid, jnp.clip((rows_left + ROW_BLOCK - 1) // ROW_BLOCK, 0, sub), 0).astype(jnp.int32)
    seg_row = (tile_start * MOE_TM).astype(jnp.int32)
    sb = jnp.arange(n_tiles * sub, dtype=jnp.int32)
    sb_e = jnp.repeat(tile_e, sub)
    sb_n = take(n, sb_e)
    sb_r0 = sb * ROW_BLOCK - take(seg_row, sb_e)
    sb_valid = ((sb // sub) < n_valid) & (sb_r0 < sb_n)
    r1 = jnp.minimum(sb_r0 + ROW_BLOCK, sb_n)
    lo = count(take(cnta.T, sb_e) <= sb_r0[:, None], 1)
    hi = count(take(cntb.T, sb_e) < r1[:, None], 1) - 1
    sb_lo = jnp.where(sb_valid, lo, 0)
    sb_hi = jnp.where(sb_valid, hi, -1)
    comb_b0 = ((seg_row[None, :] + cntb) // ROW_BLOCK).astype(jnp.int32).reshape(-1)
    return dict(tile_e=tile_e, tile_nsb=tile_nsb, seg_row=seg_row, sb_e=sb_e, sb_r0=sb_r0.astype(jnp.int32),
                sb_lo=sb_lo, sb_hi=sb_hi, comb_b0=comb_b0, n_tiles=n_tiles)


def _dispatch_kernel(e_ref, r0_ref, lo_ref, hi_ref, hn_ref, comb_ref, rankt_ref, xs_ref, gs_ref, accx_ref, accg_ref,
                     *, nch):
    sb = pl.program_id(0)
    e = e_ref[sb]
    rowid = (lax.broadcasted_iota(jnp.int32, (ROW_BLOCK, TOK_CHUNK), 0) + r0_ref[sb]).astype(F32)
    accx_ref[...] = jnp.zeros_like(accx_ref)
    accg_ref[...] = jnp.zeros_like(accg_ref)

    def body(c, _):
        c0 = pl.multiple_of(c * TOK_CHUNK, TOK_CHUNK)
        p = rowid == rankt_ref[pl.ds(e * nch + c, 1), :]
        accx_ref[...] += _dot(jnp.where(p, 1.0, 0.0).astype(BF16), hn_ref[pl.ds(c0, TOK_CHUNK), :])
        accg_ref[...] += _dot_hi(jnp.where(p, 1.0, 0.0), comb_ref[pl.ds(c0, TOK_CHUNK), :])
        return 0

    lax.fori_loop(lo_ref[sb], hi_ref[sb] + 1, body, 0)
    xs_ref[...] = accx_ref[...].astype(BF16)
    lane = lax.broadcasted_iota(jnp.int32, accg_ref.shape, 1)
    g = jnp.sum(jnp.where(lane == e, accg_ref[...], 0.0), axis=1, keepdims=True)
    gs_ref[...] = jnp.broadcast_to(g, gs_ref.shape)


def _dispatch(plan, hn, comb, rankt):
    t = hn.shape[0]
    nch = t // TOK_CHUNK
    nsb = plan['n_tiles'] * (MOE_TM // ROW_BLOCK)
    whole = pl.BlockSpec(memory_space=pltpu.VMEM)
    return pl.pallas_call(
        functools.partial(_dispatch_kernel, nch=nch),
        grid_spec=pltpu.PrefetchScalarGridSpec(
            num_scalar_prefetch=4,
            grid=(nsb,),
            in_specs=[whole, whole, whole],
            out_specs=[pl.BlockSpec((ROW_BLOCK, D_MODEL), lambda s, *_: (s, 0)),
                       pl.BlockSpec((ROW_BLOCK, LANES), lambda s, *_: (s, 0))],
            scratch_shapes=[pltpu.VMEM((ROW_BLOCK, D_MODEL), F32), pltpu.VMEM((ROW_BLOCK, LANES), F32)]),
        out_shape=[jax.ShapeDtypeStruct((nsb * ROW_BLOCK, D_MODEL), BF16),
                   jax.ShapeDtypeStruct((nsb * ROW_BLOCK, LANES), F32)],
        compiler_params=_params("arbitrary"),
        name="dispatch",
    )(plan['sb_e'], plan['sb_r0'], plan['sb_lo'], plan['sb_hi'], hn, comb, rankt.reshape(N_EXPERTS * nch, TOK_CHUNK))


def _moe_kernel(te_ref, nsb_ref, x_ref, gs_ref, wg_ref, wu_ref, wd_ref, y_ref, acc_ref, wgb_ref, wub_ref, wdb_ref):
    m = pl.program_id(0)
    f = pl.program_id(1)
    nsb = nsb_ref[m]
    sub = MOE_TM // ROW_BLOCK

    @pl.when(f == 0)
    def _():
        acc_ref[...] = jnp.zeros_like(acc_ref)

    def block(rows, wg, wu, wd):
        x = x_ref[rows, :]
        a = _silu(_dot(x, wg)) * _dot(x, wu)
        acc_ref[rows, :] += _dot(a.astype(BF16), wd)

    @pl.when(nsb == sub)
    def _():
        block(slice(None), wg_ref[...].astype(BF16), wu_ref[...].astype(BF16), wd_ref[...].astype(BF16))

    @pl.when((nsb > 0) & (nsb < sub))
    def _():
        wgb_ref[...] = wg_ref[...].astype(BF16)
        wub_ref[...] = wu_ref[...].astype(BF16)
        wdb_ref[...] = wd_ref[...].astype(BF16)

        def body(s, _):
            block(pl.ds(pl.multiple_of(s * ROW_BLOCK, ROW_BLOCK), ROW_BLOCK), wgb_ref[...], wub_ref[...], wdb_ref[...])
            return 0
        lax.fori_loop(0, nsb, body, 0)

    @pl.when(f == pl.num_programs(1) - 1)
    def _():
        y_ref[...] = (acc_ref[...] * gs_ref[:, 0:1]).astype(y_ref.dtype)


def _moe(plan, xs, gs, wg, wu, wd, tf=512):
    n_tiles = plan['n_tiles']
    nf = D_FF // tf

    def fsel(m, f, nsb):
        return jnp.where(nsb[m] > 0, f, nf - 1)

    return pl.pallas_call(
        _moe_kernel,
        grid_spec=pltpu.PrefetchScalarGridSpec(
            num_scalar_prefetch=2,
            grid=(n_tiles, nf),
            in_specs=[pl.BlockSpec((MOE_TM, D_MODEL), lambda m, f, te, nsb: (m, 0)),
                      pl.BlockSpec((MOE_TM, LANES), lambda m, f, te, nsb: (m, 0)),
                      pl.BlockSpec((None, D_MODEL, tf), lambda m, f, te, nsb: (te[m], 0, fsel(m, f, nsb))),
                      pl.BlockSpec((None, D_MODEL, tf), lambda m, f, te, nsb: (te[m], 0, fsel(m, f, nsb))),
                      pl.BlockSpec((None, tf, D_MODEL), lambda m, f, te, nsb: (te[m], fsel(m, f, nsb), 0))],
            out_specs=pl.BlockSpec((MOE_TM, D_MODEL), lambda m, f, te, nsb: (m, 0)),
            scratch_shapes=[pltpu.VMEM((MOE_TM, D_MODEL), F32),
                            pltpu.VMEM((D_MODEL, tf), BF16), pltpu.VMEM((D_MODEL, tf), BF16),
                            pltpu.VMEM((tf, D_MODEL), BF16)]),
        out_shape=jax.ShapeDtypeStruct((n_tiles * MOE_TM, D_MODEL), BF16),
        compiler_params=_params("arbitrary", "arbitrary"),
        name="moe",
    )(plan['tile_e'], plan['tile_nsb'], xs, gs, wg, wu, wd)


def _combine_kernel(b0_ref, seg_ref, h_ref, rank_ref, y0_ref, y1_ref, fw_ref, o_ref):
    c = pl.program_id(0)
    e = pl.program_id(1)

    @pl.when(e == 0)
    def _():
        o_ref[...] = h_ref[...]

    rk = rank_ref[...]
    lane = lax.broadcasted_iota(jnp.int32, rk.shape, 1)
    r = jnp.sum(jnp.where(lane == e, rk, 0.0), axis=1, keepdims=True)
    shift = (seg_ref[e] - b0_ref[c * N_EXPERTS + e] * ROW_BLOCK).astype(F32)
    loc = jnp.where(r < 0.0, -1.0, r + shift)
    col = lax.broadcasted_iota(jnp.int32, (TOK_CHUNK, ROW_BLOCK), 1).astype(F32)
    q0 = jnp.where(loc == col, 1.0, 0.0).astype(BF16)
    q1 = jnp.where(loc - ROW_BLOCK == col, 1.0, 0.0).astype(BF16)
    o_ref[...] += _dot(q0, y0_ref[...]) + _dot(q1, y1_ref[...])

    @pl.when(e == pl.num_programs(1) - 1)
    def _():
        o_ref[...] = _rms(o_ref[...], fw_ref[...])


def _combine(plan, h, rank, ys, fw):
    t = h.shape[0]
    nch = t // TOK_CHUNK
    nblk = ys.shape[0] // ROW_BLOCK
    return pl.pallas_call(
        _combine_kernel,
        grid_spec=pltpu.PrefetchScalarGridSpec(
            num_scalar_prefetch=2,
            grid=(nch, N_EXPERTS),
            in_specs=[pl.BlockSpec((TOK_CHUNK, D_MODEL), lambda c, e, b0, seg: (c, 0)),
                      pl.BlockSpec((TOK_CHUNK, LANES), lambda c, e, b0, seg: (c, 0)),
                      pl.BlockSpec((ROW_BLOCK, D_MODEL), lambda c, e, b0, seg: (b0[c * N_EXPERTS + e], 0)),
                      pl.BlockSpec((ROW_BLOCK, D_MODEL),
                                   lambda c, e, b0, seg: (jnp.minimum(b0[c * N_EXPERTS + e] + 1, nblk - 1), 0)),
                      pl.BlockSpec((1, D_MODEL), lambda c, e, b0, seg: (0, 0))],
            out_specs=pl.BlockSpec((TOK_CHUNK, D_MODEL), lambda c, e, b0, seg: (c, 0))),
        out_shape=jax.ShapeDtypeStruct((t, D_MODEL), F32),
        compiler_params=_params("arbitrary", "arbitrary"),
        name="combine",
    )(plan['comb_b0'], plan['seg_row'], h, rank, ys, ys, fw)


def _rope_tables(seqlen):
    pos = jnp.arange(seqlen, dtype=F32)
    inv_freq = 1.0 / (ROPE_THETA ** (jnp.arange(0, HEAD_DIM, 2, dtype=F32) / HEAD_DIM))
    ang = pos[:, None] * inv_freq[None, :]
    cos, sin = jnp.cos(ang), jnp.sin(ang)
    return jnp.concatenate([cos, cos], axis=-1), jnp.concatenate([-sin, sin], axis=-1)


def _arrange_w_in(w):
    q_k_v = w[:, :3 * ATTN_WIDTH]
    z = w[:, 3072:3584]
    xbc = w[:, 3584:4608]
    dt = jnp.pad(w[:, 4608:4616], ((0, 0), (0, LANES - SSM_HEADS)))
    glu = w[:, 4616:5640]
    return jnp.concatenate([q_k_v, xbc, glu, z, dt], axis=1).astype(BF16)


def kernel(x, norm_mix, w_in, ssm_conv_w, ssm_conv_b, ssm_dt_bias, ssm_a_log, ssm_d, ssm_norm_w, cf_conv_w, cf_conv_b, cf_ln_w, cf_ln_b, w_out, norm_ffn, ffn_w_gate, ffn_w_up, ffn_w_down, moe_router, moe_w_gate, moe_w_up, moe_w_down, norm_final):
    bsz, seqlen, _ = x.shape
    depth = w_in.shape[0]
    assert depth == 2 and ffn_w_gate.shape[0] == 1 and moe_router.shape[0] == 1
    cosf, sinf = _rope_tables(seqlen)
    h = x.reshape(bsz * seqlen, D_MODEL)
    for layer in range(depth):
        proj = _inproj(h, norm_mix[layer].reshape(1, -1), _arrange_w_in(w_in[layer]))
        attn = _moba(proj, cosf, sinf, bsz, seqlen)
        ssm = _ssd(proj, ssm_conv_w[layer], ssm_conv_b[layer], ssm_dt_bias[layer], ssm_a_log[layer],
                   ssm_d[layer], ssm_norm_w[layer], bsz, seqlen)
        conv = _conformer(proj, cf_conv_w[layer], cf_conv_b[layer], cf_ln_w[layer], cf_ln_b[layer], bsz, seqlen)
        h = _outproj(attn, ssm, conv, w_out[layer].astype(BF16), h)
        nw = norm_ffn[layer].reshape(1, -1)
        if layer % 2 == 0:
            i = layer // 2
            h = _ffn(h, nw, ffn_w_gate[i].astype(BF16), ffn_w_up[i].astype(BF16), ffn_w_down[i].astype(BF16))
        else:
            i = layer // 2
            hn, comb, rank, rankt, cnt_after = _router(h, nw, moe_router[i])
            plan = _moe_plan(cnt_after, h.shape[0])
            xs, gs = _dispatch(plan, hn, comb, rankt)
            ys = _moe(plan, xs, gs, moe_w_gate[i], moe_w_up[i], moe_w_down[i])
            h = _combine(plan, h, rank, ys, norm_final.reshape(1, -1))
    return h.reshape(bsz, seqlen, D_MODEL)
```

```python
import functools
import math

import jax
import jax.numpy as jnp
from jax import lax
from jax.experimental import pallas as pl
from jax.experimental.pallas import tpu as pltpu

F32 = jnp.float32
BF16 = jnp.bfloat16

D_MODEL = 2048
HEAD_DIM = 128
ATTN_WIDTH = 1024
ATTN_HEADS = 8
MOBA_BLOCK = 256
MOBA_TOPK = 3
ROPE_THETA = 10000.0
SSM_WIDTH = 512
SSM_HEAD_DIM = 64
SSM_HEADS = 8
SSM_GROUPS = 2
SSM_STATE = 128
SSM_CONV = 4
SSM_CHUNK = 128
SSM_CONV_DIM = 1024
CONV_WIDTH = 512
CONV_SIZE = 31
D_FF = 7168
N_EXPERTS = 8
EPS = 1e-5

LANES = 128
NEG = -1e30

COL_Q, COL_K, COL_V = 0, 1024, 2048
COL_XBC, COL_GLU, COL_Z, COL_DT = 3072, 4096, 5120, 5632
IN_PAD = 5760

VMEM_LIMIT = 56 * 1024 * 1024


def _params(*sem):
    return pltpu.CompilerParams(dimension_semantics=sem, vmem_limit_bytes=VMEM_LIMIT)


def _dot(a, b):
    return jnp.dot(a, b, preferred_element_type=F32)


def _dot_nt(a, b):
    return lax.dot_general(a, b, (((1,), (1,)), ((), ())), preferred_element_type=F32)


def _dot_hi(a, b):
    return jnp.dot(a, b, preferred_element_type=F32, precision=lax.Precision.HIGHEST)


def _dot_nt_hi(a, b):
    return lax.dot_general(a, b, (((1,), (1,)), ((), ())), preferred_element_type=F32,
                           precision=lax.Precision.HIGHEST)


def _rms(x, w):
    return x * lax.rsqrt(jnp.mean(x * x, axis=-1, keepdims=True) + EPS) * w


def _silu(x):
    return x * jax.nn.sigmoid(x)


def _inproj_kernel(x_ref, nw_ref, w_ref, o_ref, xn_ref):
    @pl.when(pl.program_id(1) == 0)
    def _():
        xn_ref[...] = _rms(x_ref[...], nw_ref[...]).astype(BF16)

    o_ref[...] = _dot(xn_ref[...], w_ref[...])


def _inproj(h, nw, w, tm=1024, tn=1152):
    t = h.shape[0]
    return pl.pallas_call(
        _inproj_kernel,
        grid=(t // tm, IN_PAD // tn),
        in_specs=[pl.BlockSpec((tm, D_MODEL), lambda i, j: (i, 0)),
                  pl.BlockSpec((1, D_MODEL), lambda i, j: (0, 0)),
                  pl.BlockSpec((D_MODEL, tn), lambda i, j: (0, j))],
        out_specs=pl.BlockSpec((tm, tn), lambda i, j: (i, j)),
        out_shape=jax.ShapeDtypeStruct((t, IN_PAD), F32),
        scratch_shapes=[pltpu.VMEM((tm, D_MODEL), BF16)],
        compiler_params=_params("parallel", "arbitrary"),
        name="inproj",
    )(h, nw, w)


def _rope(x, cosf, sinf):
    return x * cosf + pltpu.roll(x, HEAD_DIM // 2, 1) * sinf


MOBA_HP = 2


def _moba_kernel(q_ref, k_ref, v_ref, cos_ref, sin_ref, o_ref, kr_ref, vb_ref, km_ref, *, nb):
    i = pl.program_id(2)
    blk = MOBA_BLOCK
    heads = [slice(hh * HEAD_DIM, (hh + 1) * HEAD_DIM) for hh in range(MOBA_HP)]

    @pl.when(i == 0)
    def _():
        km_ref[...] = jnp.zeros_like(km_ref)
        vb_ref[...] = v_ref[...].astype(BF16)
        for hh, hs in enumerate(heads):
            for j in range(nb):
                sl = slice(j * blk, (j + 1) * blk)
                kr = _rope(k_ref[sl, hs], cos_ref[sl, :], sin_ref[sl, :])
                kr_ref[sl, hs] = kr.astype(BF16)
                km_ref[hh, j:j + 1, :] = jnp.mean(kr, axis=0, keepdims=True)

    row0 = pl.multiple_of(i * blk, blk)
    rows = pl.ds(row0, blk)
    state, init = [], []
    for hh, hs in enumerate(heads):
        q = _rope(q_ref[:, hs], cos_ref[rows, :], sin_ref[rows, :])
        gate = _dot_nt_hi(q, km_ref[hh])
        lane = lax.broadcasted_iota(jnp.int32, gate.shape, 1)
        gate = jnp.where(lane < i, gate, -jnp.inf)
        cnt = jnp.zeros(gate.shape, jnp.int32)
        for c in range(nb - 1):
            gc = gate[:, c:c + 1]
            beats = (gc > gate) | ((gc == gate) & (c < lane))
            cnt = cnt + jnp.where(beats, 1, 0)
        selbias = jnp.where((cnt < MOBA_TOPK) & (lane < i), 0.0, NEG)

        qs = (q * (HEAD_DIM ** -0.5)).astype(BF16)
        s = _dot_nt(qs, kr_ref[rows, hs])
        qpos = lax.broadcasted_iota(jnp.int32, s.shape, 0)
        kpos = lax.broadcasted_iota(jnp.int32, s.shape, 1)
        s = jnp.where(kpos <= qpos, s, NEG)
        m0 = jnp.max(s, axis=1, keepdims=True)
        p = jnp.exp(s - m0)
        l0 = jnp.sum(p, axis=1, keepdims=True)
        acc0 = _dot(p.astype(BF16), vb_ref[rows, hs])
        state.append((hs, lane, selbias, qs))
        init.append((m0, l0, acc0))

    def body(j, carry):
        keys = pl.ds(pl.multiple_of(j * blk, blk), blk)
        out = []
        for (hs, lane, selbias, qs), (m, l, acc) in zip(state, carry):
            bias = jnp.max(jnp.where(lane == j, selbias, NEG), axis=1, keepdims=True)
            sj = _dot_nt(qs, kr_ref[keys, hs]) + bias
            mn = jnp.maximum(m, jnp.max(sj, axis=1, keepdims=True))
            alpha = jnp.exp(m - mn)
            pj = jnp.exp(sj - mn)
            l = alpha * l + jnp.sum(pj, axis=1, keepdims=True)
            acc = alpha * acc + _dot(pj.astype(BF16), vb_ref[keys, hs])
            out.append((mn, l, acc))
        return tuple(out)

    final = lax.fori_loop(0, i, body, tuple(init))
    for (hs, _, _, _), (_, l, acc) in zip(state, final):
        o_ref[:, hs] = (acc / l).astype(o_ref.dtype)


def _moba(proj, cosf, sinf, bsz, seqlen):
    nb = seqlen // MOBA_BLOCK
    hw = MOBA_HP * HEAD_DIM
    return pl.pallas_call(
        functools.partial(_moba_kernel, nb=nb),
        grid=(bsz, ATTN_HEADS // MOBA_HP, nb),
        in_specs=[pl.BlockSpec((MOBA_BLOCK, hw), lambda b, h, i: (b * nb + i, COL_Q // hw + h)),
                  pl.BlockSpec((seqlen, hw), lambda b, h, i: (b, COL_K // hw + h)),
                  pl.BlockSpec((seqlen, hw), lambda b, h, i: (b, COL_V // hw + h)),
                  pl.BlockSpec((seqlen, HEAD_DIM), lambda b, h, i: (0, 0)),
                  pl.BlockSpec((seqlen, HEAD_DIM), lambda b, h, i: (0, 0))],
        out_specs=pl.BlockSpec((MOBA_BLOCK, hw), lambda b, h, i: (b * nb + i, h)),
        out_shape=jax.ShapeDtypeStruct((bsz * seqlen, ATTN_WIDTH), BF16),
        scratch_shapes=[pltpu.VMEM((seqlen, hw), BF16),
                        pltpu.VMEM((seqlen, hw), BF16),
                        pltpu.VMEM((MOBA_HP, LANES, HEAD_DIM), F32)],
        compiler_params=_params("parallel", "parallel", "arbitrary"),
        name="moba",
    )(proj, proj, proj, cosf, sinf)


def _ssd_kernel(xbc_ref, z_ref, dt_ref, cw_ref, cb_ref, dtb_ref, alog_ref, dch_ref, nw_ref, ex_ref,
                o_ref, xpad_ref, st_ref, y_ref):
    c = pl.program_id(1)
    cl = SSM_CHUNK
    hd = SSM_HEAD_DIM
    gw = SSM_WIDTH // SSM_GROUPS
    hg = SSM_HEADS // SSM_GROUPS

    @pl.when(c == 0)
    def _():
        xpad_ref[0:8, :] = jnp.zeros((8, SSM_CONV_DIM), F32)
        st_ref[...] = jnp.zeros_like(st_ref)

    @pl.when(c > 0)
    def _():
        xpad_ref[0:8, :] = xpad_ref[cl:cl + 8, :]

    xpad_ref[8:8 + cl, :] = xbc_ref[...]
    conv = jnp.zeros((cl, SSM_CONV_DIM), F32) + cb_ref[...]
    for k in range(SSM_CONV):
        off = 8 - (SSM_CONV - 1) + k
        conv = conv + xpad_ref[off:off + cl, :] * cw_ref[k:k + 1, :]
    act = _silu(conv)
    xs = act[:, :SSM_WIDTH]

    dt = jax.nn.softplus(dt_ref[...] + dtb_ref[...])
    la = dt * (-jnp.exp(alog_ref[...]))
    ti = lax.broadcasted_iota(jnp.int32, (cl, cl), 0)
    si = lax.broadcasted_iota(jnp.int32, (cl, cl), 1)
    causal = si <= ti
    cum = _dot_hi(jnp.where(causal, 1.0, 0.0), la)
    cum_t = cum.T
    ex = ex_ref[...]
    dt_c = _dot_hi(dt, ex)
    cum_c = _dot_hi(cum, ex)
    cum_last = cum_c[cl - 1:cl, :]
    xr = xs * dt_c
    xw = xr * jnp.exp(cum_last - cum_c)
    e_cum = jnp.exp(cum_c)
    e_last = jnp.exp(cum_last)

    for g in range(SSM_GROUPS):
        bm = act[:, SSM_WIDTH + g * SSM_STATE:SSM_WIDTH + (g + 1) * SSM_STATE]
        cm = act[:, SSM_WIDTH + (SSM_GROUPS + g) * SSM_STATE:SSM_WIDTH + (SSM_GROUPS + g + 1) * SSM_STATE]
        bm_t = bm.T
        cb = _dot(cm, bm_t)
        gs = slice(g * gw, (g + 1) * gw)
        st = st_ref[g]
        y_ref[:, gs] = _dot(cm, st) * e_cum[:, gs]
        st_ref[g] = st * e_last[:, gs] + _dot(bm_t, xw[:, gs])
        for hh in range(hg):
            h = g * hg + hh
            seg = cum[:, h:h + 1] - cum_t[h:h + 1, :]
            decay = jnp.exp(jnp.where(causal, seg, -jnp.inf))
            hs = slice(h * hd, (h + 1) * hd)
            y_ref[:, hs] = y_ref[:, hs] + _dot(cb * decay, xr[:, hs])

    y = (y_ref[...] + dch_ref[...] * xs) * _silu(z_ref[...])
    for g in range(SSM_GROUPS):
        gs = slice(g * gw, (g + 1) * gw)
        yg = y[:, gs]
        yg = yg * lax.rsqrt(jnp.mean(yg * yg, axis=-1, keepdims=True) + EPS)
        o_ref[:, gs] = (yg * nw_ref[:, gs]).astype(o_ref.dtype)


def _ssd(proj, conv_w, conv_b, dt_bias, a_log, d_skip, norm_w, bsz, seqlen):
    nc = seqlen // SSM_CHUNK
    pad = LANES - SSM_HEADS
    dtb = jnp.pad(dt_bias, (0, pad)).reshape(1, LANES)
    alog = jnp.pad(a_log, (0, pad)).reshape(1, LANES)
    dch = jnp.repeat(d_skip, SSM_HEAD_DIM).reshape(1, SSM_WIDTH)
    expand = (jnp.arange(LANES)[:, None] == (jnp.arange(SSM_WIDTH)[None, :] // SSM_HEAD_DIM)).astype(F32)
    const = lambda shape: pl.BlockSpec(shape, lambda b, c: (0, 0))
    return pl.pallas_call(
        _ssd_kernel,
        grid=(bsz, nc),
        in_specs=[pl.BlockSpec((SSM_CHUNK, SSM_CONV_DIM), lambda b, c: (b * nc + c, COL_XBC // SSM_CONV_DIM)),
                  pl.BlockSpec((SSM_CHUNK, SSM_WIDTH), lambda b, c: (b * nc + c, COL_Z // SSM_WIDTH)),
                  pl.BlockSpec((SSM_CHUNK, LANES), lambda b, c: (b * nc + c, COL_DT // LANES)),
                  const((SSM_CONV, SSM_CONV_DIM)), const((1, SSM_CONV_DIM)),
                  const((1, LANES)), const((1, LANES)), const((1, SSM_WIDTH)), const((1, SSM_WIDTH)),
                  const((LANES, SSM_WIDTH))],
        out_specs=pl.BlockSpec((SSM_CHUNK, SSM_WIDTH), lambda b, c: (b * nc + c, 0)),
        out_shape=jax.ShapeDtypeStruct((bsz * seqlen, SSM_WIDTH), BF16),
        scratch_shapes=[pltpu.VMEM((SSM_CHUNK + 8, SSM_CONV_DIM), F32),
                        pltpu.VMEM((SSM_GROUPS, SSM_STATE, SSM_WIDTH // SSM_GROUPS), F32),
                        pltpu.VMEM((SSM_CHUNK, SSM_WIDTH), F32)],
        compiler_params=_params("parallel", "arbitrary"),
        name="ssd",
    )(proj, proj, proj, conv_w, conv_b.reshape(1, -1), dtb, alog, dch, norm_w.reshape(1, -1), expand)


CF_TILE = 256
CF_HALO = 32
CF_ROWS = 64


def _conformer_kernel(u_ref, cw_ref, cb_ref, lw_ref, lb_ref, o_ref, buf_ref, cv_ref):
    i = pl.program_id(1)
    ts = CF_TILE

    @pl.when(i == 0)
    def _():
        buf_ref[0:CF_HALO, :] = jnp.zeros((CF_HALO, CONV_WIDTH), F32)

    @pl.when(i > 0)
    def _():
        buf_ref[0:CF_HALO, :] = buf_ref[ts:ts + CF_HALO, :]

    buf_ref[CF_HALO:CF_HALO + ts, :] = u_ref[:, :CONV_WIDTH] * jax.nn.sigmoid(u_ref[:, CONV_WIDTH:])
    base = CF_HALO - (CONV_SIZE - 1)
    for r in range(ts // CF_ROWS):
        for cc in range(CONV_WIDTH // LANES):
            cs = slice(cc * LANES, (cc + 1) * LANES)
            acc = jnp.zeros((CF_ROWS, LANES), F32) + cb_ref[:, cs]
            for k in range(CONV_SIZE):
                off = r * CF_ROWS + base + k
                acc = acc + buf_ref[off:off + CF_ROWS, cs] * cw_ref[k:k + 1, cs]
            cv_ref[r * CF_ROWS:(r + 1) * CF_ROWS, cs] = acc
    hf = cv_ref[...]
    mu = jnp.mean(hf, axis=-1, keepdims=True)
    d = hf - mu
    var = jnp.mean(d * d, axis=-1, keepdims=True)
    o_ref[...] = _silu(d * lax.rsqrt(var + EPS) * lw_ref[...] + lb_ref[...]).astype(o_ref.dtype)


def _conformer(proj, conv_w, conv_b, ln_w, ln_b, bsz, seqlen):
    nt = seqlen // CF_TILE
    const = lambda shape: pl.BlockSpec(shape, lambda b, i: (0, 0))
    return pl.pallas_call(
        _conformer_kernel,
        grid=(bsz, nt),
        in_specs=[pl.BlockSpec((CF_TILE, 2 * CONV_WIDTH), lambda b, i: (b * nt + i, COL_GLU // (2 * CONV_WIDTH))),
                  const((CONV_SIZE, CONV_WIDTH)), const((1, CONV_WIDTH)),
                  const((1, CONV_WIDTH)), const((1, CONV_WIDTH))],
        out_specs=pl.BlockSpec((CF_TILE, CONV_WIDTH), lambda b, i: (b * nt + i, 0)),
        out_shape=jax.ShapeDtypeStruct((bsz * seqlen, CONV_WIDTH), BF16),
        scratch_shapes=[pltpu.VMEM((CF_HALO + CF_TILE, CONV_WIDTH), F32),
                        pltpu.VMEM((CF_TILE, CONV_WIDTH), F32)],
        compiler_params=_params("parallel", "arbitrary"),
        name="conformer",
    )(proj, conv_w, conv_b.reshape(1, -1), ln_w.reshape(1, -1), ln_b.reshape(1, -1))


def _outproj_kernel(a_ref, s_ref, c_ref, wa_ref, ws_ref, wc_ref, h_ref, o_ref):
    o_ref[...] = (h_ref[...] + _dot(a_ref[...], wa_ref[...]) + _dot(s_ref[...], ws_ref[...])
                  + _dot(c_ref[...], wc_ref[...]))


def _outproj(attn, ssm, conv, w_out, h, tm=512):
    t = h.shape[0]
    row = lambda w: pl.BlockSpec((tm, w), lambda i: (i, 0))
    return pl.pallas_call(
        _outproj_kernel,
        grid=(t // tm,),
        in_specs=[row(ATTN_WIDTH), row(SSM_WIDTH), row(CONV_WIDTH),
                  pl.BlockSpec((ATTN_WIDTH, D_MODEL), lambda i: (0, 0)),
                  pl.BlockSpec((SSM_WIDTH, D_MODEL), lambda i: (ATTN_WIDTH // SSM_WIDTH, 0)),
                  pl.BlockSpec((CONV_WIDTH, D_MODEL), lambda i: ((ATTN_WIDTH + SSM_WIDTH) // CONV_WIDTH, 0)),
                  row(D_MODEL)],
        out_specs=row(D_MODEL),
        out_shape=jax.ShapeDtypeStruct((t, D_MODEL), F32),
        compiler_params=_params("parallel"),
        name="outproj",
    )(attn, ssm, conv, w_out, w_out, w_out, h)


def _ffn_kernel(h_ref, nw_ref, wg_ref, wu_ref, wd_ref, o_ref, hn_ref):
    @pl.when(pl.program_id(1) == 0)
    def _():
        h = h_ref[...]
        hn_ref[...] = _rms(h, nw_ref[...]).astype(BF16)
        o_ref[...] = h

    hn = hn_ref[...]
    a = _silu(_dot(hn, wg_ref[...])) * _dot(hn, wu_ref[...])
    o_ref[...] += _dot(a.astype(BF16), wd_ref[...])


def _ffn(h, nw, wg, wu, wd, tm=1024, tf=512):
    t = h.shape[0]
    return pl.pallas_call(
        _ffn_kernel,
        grid=(t // tm, D_FF // tf),
        in_specs=[pl.BlockSpec((tm, D_MODEL), lambda i, f: (i, 0)),
                  pl.BlockSpec((1, D_MODEL), lambda i, f: (0, 0)),
                  pl.BlockSpec((D_MODEL, tf), lambda i, f: (0, f)),
                  pl.BlockSpec((D_MODEL, tf), lambda i, f: (0, f)),
                  pl.BlockSpec((tf, D_MODEL), lambda i, f: (f, 0))],
        out_specs=pl.BlockSpec((tm, D_MODEL), lambda i, f: (i, 0)),
        out_shape=jax.ShapeDtypeStruct((t, D_MODEL), F32),
        scratch_shapes=[pltpu.VMEM((tm, D_MODEL), BF16)],
        compiler_params=_params("parallel", "arbitrary"),
        name="ffn",
    )(h, nw, wg, wu, wd)


TOK_CHUNK = 256
ROW_BLOCK = 256
MOE_TM = 768


def _router_kernel(h_ref, nw_ref, wr_ref, hn_ref, comb_ref, rank_ref, rankt_ref, cnt_ref, carry_ref):
    @pl.when(pl.program_id(0) == 0)
    def _():
        carry_ref[...] = jnp.zeros_like(carry_ref)

    hn = _rms(h_ref[...], nw_ref[...])
    hn_ref[...] = hn.astype(BF16)
    logits = _dot_hi(hn, wr_ref[...])
    lane = lax.broadcasted_iota(jnp.int32, logits.shape, 1)
    logits = jnp.where(lane < N_EXPERTS, logits, -jnp.inf)
    m1 = jnp.max(logits, axis=1, keepdims=True)
    i1 = jnp.min(jnp.where(logits == m1, lane, LANES), axis=1, keepdims=True)
    rest = jnp.where(lane == i1, -jnp.inf, logits)
    m2 = jnp.max(rest, axis=1, keepdims=True)
    i2 = jnp.min(jnp.where(rest == m2, lane, LANES), axis=1, keepdims=True)
    e2 = jnp.exp(m2 - m1)
    den = 1.0 + e2
    comb_ref[...] = jnp.where(lane == i1, 1.0 / den, 0.0) + jnp.where(lane == i2, e2 / den, 0.0)
    sel = (lane == i1) | (lane == i2)
    self = jnp.where(sel, 1.0, 0.0)
    n = logits.shape[0]
    tri = jnp.where(lax.broadcasted_iota(jnp.int32, (n, n), 1) <= lax.broadcasted_iota(jnp.int32, (n, n), 0),
                    1.0, 0.0).astype(BF16)
    incl = _dot(tri, self.astype(BF16))
    carry = carry_ref[0:1, :]
    rank = jnp.where(sel, carry + incl - self, -1.0)
    rank_ref[...] = rank
    rankt_ref[...] = rank.T[:N_EXPERTS, :]
    carry_ref[...] = jnp.broadcast_to(carry + incl[n - 1:n, :], carry_ref.shape)
    cnt_ref[...] = carry_ref[...]


def _router(h, nw, wr):
    t = h.shape[0]
    nch = t // TOK_CHUNK
    wr_p = jnp.pad(wr, ((0, 0), (0, LANES - N_EXPERTS)))
    return pl.pallas_call(
        _router_kernel,
        grid=(nch,),
        in_specs=[pl.BlockSpec((TOK_CHUNK, D_MODEL), lambda i: (i, 0)),
                  pl.BlockSpec((1, D_MODEL), lambda i: (0, 0)),
                  pl.BlockSpec((D_MODEL, LANES), lambda i: (0, 0))],
        out_specs=[pl.BlockSpec((TOK_CHUNK, D_MODEL), lambda i: (i, 0)),
                   pl.BlockSpec((TOK_CHUNK, LANES), lambda i: (i, 0)),
                   pl.BlockSpec((TOK_CHUNK, LANES), lambda i: (i, 0)),
                   pl.BlockSpec((N_EXPERTS, TOK_CHUNK), lambda i: (0, i)),
                   pl.BlockSpec((None, 8, LANES), lambda i: (i, 0, 0))],
        out_shape=[jax.ShapeDtypeStruct((t, D_MODEL), BF16),
                   jax.ShapeDtypeStruct((t, LANES), F32),
                   jax.ShapeDtypeStruct((t, LANES), F32),
                   jax.ShapeDtypeStruct((N_EXPERTS, t), F32),
                   jax.ShapeDtypeStruct((nch, 8, LANES), F32)],
        scratch_shapes=[pltpu.VMEM((8, LANES), F32)],
        compiler_params=_params("arbitrary"),
        name="router",
    )(h, nw, wr_p)


def _moe_plan(cnt_after, t):
    nch = t // TOK_CHUNK
    n_tiles = (t * 2) // MOE_TM + N_EXPERTS
    sub = MOE_TM // ROW_BLOCK
    cnta = cnt_after[:, 0, :N_EXPERTS].astype(jnp.int32)
    cntb = jnp.concatenate([jnp.zeros((1, N_EXPERTS), jnp.int32), cnta[:-1]], axis=0)
    n = cnta[-1]
    tiles_e = (n + MOE_TM - 1) // MOE_TM
    tile_end = jnp.cumsum(tiles_e)
    tile_start = tile_end - tiles_e
    n_valid = tile_end[-1]
    eid = jnp.arange(N_EXPERTS, dtype=jnp.int32)

    def count(cond, axis):
        return jnp.sum(cond.astype(jnp.int32), axis=axis)

    def take(table, idx):
        hot = (idx[:, None] == eid[None, :]).astype(jnp.int32)
        if table.ndim == 1:
            return jnp.sum(hot * table[None, :], axis=1)
        return jnp.sum(hot[:, :, None] * table[None, :, :], axis=1)

    m = jnp.arange(n_tiles, dtype=jnp.int32)
    last_e = count(tile_end <= n_valid - 1, 0)
    tile_e = jnp.where(m < n_valid, count(tile_end[None, :] <= m[:, None], 1), last_e)
    tile_e = jnp.minimum(tile_e, N_EXPERTS - 1).astype(jnp.int32)
    rows_left = take(n, tile_e) - (m - take(tile_start, tile_e)) * MOE_TM
    tile_nsb = jnp.where(m < n_valid, jnp.clip((rows_left + ROW_BLOCK - 1) // ROW_BLOCK, 0, sub), 0).astype(jnp.int32)
    seg_row = (tile_start * MOE_TM).astype(jnp.int32)
    sb = jnp.arange(n_tiles * sub, dtype=jnp.int32)
    sb_e = jnp.repeat(tile_e, sub)
    sb_n = take(n, sb_e)
    sb_r0 = sb * ROW_BLOCK - take(seg_row, sb_e)
    sb_valid = ((sb // sub) < n_valid) & (sb_r0 < sb_n)
    r1 = jnp.minimum(sb_r0 + ROW_BLOCK, sb_n)
    lo = count(take(cnta.T, sb_e) <= sb_r0[:, None], 1)
    hi = count(take(cntb.T, sb_e) < r1[:, None], 1) - 1
    sb_lo = jnp.where(sb_valid, lo, 0)
    sb_hi = jnp.where(sb_valid, hi, -1)
    comb_b0 = ((seg_row[None, :] + cntb) // ROW_BLOCK).astype(jnp.int32).reshape(-1)
    return dict(tile_e=tile_e, tile_nsb=tile_nsb, seg_row=seg_row, sb_e=sb_e, sb_r0=sb_r0.astype(jnp.int32),
                sb_lo=sb_lo, sb_hi=sb_hi, comb_b0=comb_b0, n_tiles=n_tiles)


def _dispatch_kernel(e_ref, r0_ref, lo_ref, hi_ref, hn_ref, comb_ref, rankt_ref, xs_ref, gs_ref, accx_ref, accg_ref,
                     *, nch):
    sb = pl.program_id(0)
    e = e_ref[sb]
    rowid = (lax.broadcasted_iota(jnp.int32, (ROW_BLOCK, TOK_CHUNK), 0) + r0_ref[sb]).astype(F32)
    accx_ref[...] = jnp.zeros_like(accx_ref)
    accg_ref[...] = jnp.zeros_like(accg_ref)

    def body(c, _):
        c0 = pl.multiple_of(c * TOK_CHUNK, TOK_CHUNK)
        p = rowid == rankt_ref[pl.ds(e * nch + c, 1), :]
        accx_ref[...] += _dot(jnp.where(p, 1.0, 0.0).astype(BF16), hn_ref[pl.ds(c0, TOK_CHUNK), :])
        accg_ref[...] += _dot_hi(jnp.where(p, 1.0, 0.0), comb_ref[pl.ds(c0, TOK_CHUNK), :])
        return 0

    lax.fori_loop(lo_ref[sb], hi_ref[sb] + 1, body, 0)
    xs_ref[...] = accx_ref[...].astype(BF16)
    lane = lax.broadcasted_iota(jnp.int32, accg_ref.shape, 1)
    g = jnp.sum(jnp.where(lane == e, accg_ref[...], 0.0), axis=1, keepdims=True)
    gs_ref[...] = jnp.broadcast_to(g, gs_ref.shape)


def _dispatch(plan, hn, comb, rankt):
    t = hn.shape[0]
    nch = t // TOK_CHUNK
    nsb = plan['n_tiles'] * (MOE_TM // ROW_BLOCK)
    whole = pl.BlockSpec(memory_space=pltpu.VMEM)
    return pl.pallas_call(
        functools.partial(_dispatch_kernel, nch=nch),
        grid_spec=pltpu.PrefetchScalarGridSpec(
            num_scalar_prefetch=4,
            grid=(nsb,),
            in_specs=[whole, whole, whole],
            out_specs=[pl.BlockSpec((ROW_BLOCK, D_MODEL), lambda s, *_: (s, 0)),
                       pl.BlockSpec((ROW_BLOCK, LANES), lambda s, *_: (s, 0))],
            scratch_shapes=[pltpu.VMEM((ROW_BLOCK, D_MODEL), F32), pltpu.VMEM((ROW_BLOCK, LANES), F32)]),
        out_shape=[jax.ShapeDtypeStruct((nsb * ROW_BLOCK, D_MODEL), BF16),
                   jax.ShapeDtypeStruct((nsb * ROW_BLOCK, LANES), F32)],
        compiler_params=_params("arbitrary"),
        name="dispatch",
    )(plan['sb_e'], plan['sb_r0'], plan['sb_lo'], plan['sb_hi'], hn, comb, rankt.reshape(N_EXPERTS * nch, TOK_CHUNK))


def _moe_kernel(te_ref, nsb_ref, x_ref, gs_ref, wg_ref, wu_ref, wd_ref, y_ref, acc_ref):
    m = pl.program_id(0)
    f = pl.program_id(1)
    nsb = nsb_ref[m]
    sub = MOE_TM // ROW_BLOCK

    @pl.when(f == 0)
    def _():
        acc_ref[...] = jnp.zeros_like(acc_ref)

    def block(rows):
        x = x_ref[rows, :]
        a = _silu(_dot(x, wg_ref[...].astype(BF16))) * _dot(x, wu_ref[...].astype(BF16))
        acc_ref[rows, :] += _dot(a.astype(BF16), wd_ref[...].astype(BF16))

    @pl.when(nsb == sub)
    def _():
        block(slice(None))

    @pl.when((nsb > 0) & (nsb < sub))
    def _():
        def body(s, _):
            block(pl.ds(pl.multiple_of(s * ROW_BLOCK, ROW_BLOCK), ROW_BLOCK))
            return 0
        lax.fori_loop(0, nsb, body, 0)

    @pl.when(f == pl.num_programs(1) - 1)
    def _():
        y_ref[...] = (acc_ref[...] * gs_ref[:, 0:1]).astype(y_ref.dtype)


def _moe(plan, xs, gs, wg, wu, wd, tf=512):
    n_tiles = plan['n_tiles']
    nf = D_FF // tf

    def fsel(m, f, nsb):
        return jnp.where(nsb[m] > 0, f, nf - 1)

    return pl.pallas_call(
        _moe_kernel,
        grid_spec=pltpu.PrefetchScalarGridSpec(
            num_scalar_prefetch=2,
            grid=(n_tiles, nf),
            in_specs=[pl.BlockSpec((MOE_TM, D_MODEL), lambda m, f, te, nsb: (m, 0)),
                      pl.BlockSpec((MOE_TM, LANES), lambda m, f, te, nsb: (m, 0)),
                      pl.BlockSpec((None, D_MODEL, tf), lambda m, f, te, nsb: (te[m], 0, fsel(m, f, nsb))),
                      pl.BlockSpec((None, D_MODEL, tf), lambda m, f, te, nsb: (te[m], 0, fsel(m, f, nsb))),
                      pl.BlockSpec((None, tf, D_MODEL), lambda m, f, te, nsb: (te[m], fsel(m, f, nsb), 0))],
            out_specs=pl.BlockSpec((MOE_TM, D_MODEL), lambda m, f, te, nsb: (m, 0)),
            scratch_shapes=[pltpu.VMEM((MOE_TM, D_MODEL), F32)]),
        out_shape=jax.ShapeDtypeStruct((n_tiles * MOE_TM, D_MODEL), BF16),
        compiler_params=_params("arbitrary", "arbitrary"),
        name="moe",
    )(plan['tile_e'], plan['tile_nsb'], xs, gs, wg, wu, wd)


def _combine_kernel(b0_ref, seg_ref, h_ref, rank_ref, y0_ref, y1_ref, fw_ref, o_ref):
    c = pl.program_id(0)
    e = pl.program_id(1)

    @pl.when(e == 0)
    def _():
        o_ref[...] = h_ref[...]

    rk = rank_ref[...]
    lane = lax.broadcasted_iota(jnp.int32, rk.shape, 1)
    r = jnp.sum(jnp.where(lane == e, rk, 0.0), axis=1, keepdims=True)
    shift = (seg_ref[e] - b0_ref[c * N_EXPERTS + e] * ROW_BLOCK).astype(F32)
    loc = jnp.where(r < 0.0, -1.0, r + shift)
    col = lax.broadcasted_iota(jnp.int32, (TOK_CHUNK, ROW_BLOCK), 1).astype(F32)
    q0 = jnp.where(loc == col, 1.0, 0.0).astype(BF16)
    q1 = jnp.where(loc - ROW_BLOCK == col, 1.0, 0.0).astype(BF16)
    o_ref[...] += _dot(q0, y0_ref[...]) + _dot(q1, y1_ref[...])

    @pl.when(e == pl.num_programs(1) - 1)
    def _():
        o_ref[...] = _rms(o_ref[...], fw_ref[...])


def _combine(plan, h, rank, ys, fw):
    t = h.shape[0]
    nch = t // TOK_CHUNK
    nblk = ys.shape[0] // ROW_BLOCK
    return pl.pallas_call(
        _combine_kernel,
        grid_spec=pltpu.PrefetchScalarGridSpec(
            num_scalar_prefetch=2,
            grid=(nch, N_EXPERTS),
            in_specs=[pl.BlockSpec((TOK_CHUNK, D_MODEL), lambda c, e, b0, seg: (c, 0)),
                      pl.BlockSpec((TOK_CHUNK, LANES), lambda c, e, b0, seg: (c, 0)),
                      pl.BlockSpec((ROW_BLOCK, D_MODEL), lambda c, e, b0, seg: (b0[c * N_EXPERTS + e], 0)),
                      pl.BlockSpec((ROW_BLOCK, D_MODEL),
                                   lambda c, e, b0, seg: (jnp.minimum(b0[c * N_EXPERTS + e] + 1, nblk - 1), 0)),
                      pl.BlockSpec((1, D_MODEL), lambda c, e, b0, seg: (0, 0))],
            out_specs=pl.BlockSpec((TOK_CHUNK, D_MODEL), lambda c, e, b0, seg: (c, 0))),
        out_shape=jax.ShapeDtypeStruct((t, D_MODEL), F32),
        compiler_params=_params("arbitrary", "arbitrary"),
        name="combine",
    )(plan['comb_b0'], plan['seg_row'], h, rank, ys, ys, fw)


def _rope_tables(seqlen):
    pos = jnp.arange(seqlen, dtype=F32)
    inv_freq = 1.0 / (ROPE_THETA ** (jnp.arange(0, HEAD_DIM, 2, dtype=F32) / HEAD_DIM))
    ang = pos[:, None] * inv_freq[None, :]
    cos, sin = jnp.cos(ang), jnp.sin(ang)
    return jnp.concatenate([cos, cos], axis=-1), jnp.concatenate([-sin, sin], axis=-1)


def _arrange_w_in(w):
    q_k_v = w[:, :3 * ATTN_WIDTH]
    z = w[:, 3072:3584]
    xbc = w[:, 3584:4608]
    dt = jnp.pad(w[:, 4608:4616], ((0, 0), (0, LANES - SSM_HEADS)))
    glu = w[:, 4616:5640]
    return jnp.concatenate([q_k_v, xbc, glu, z, dt], axis=1).astype(BF16)


def kernel(x, norm_mix, w_in, ssm_conv_w, ssm_conv_b, ssm_dt_bias, ssm_a_log, ssm_d, ssm_norm_w, cf_conv_w, cf_conv_b, cf_ln_w, cf_ln_b, w_out, norm_ffn, ffn_w_gate, ffn_w_up, ffn_w_down, moe_router, moe_w_gate, moe_w_up, moe_w_down, norm_final):
    bsz, seqlen, _ = x.shape
    depth = w_in.shape[0]
    assert depth == 2 and ffn_w_gate.shape[0] == 1 and moe_router.shape[0] == 1
    cosf, sinf = _rope_tables(seqlen)
    h = x.reshape(bsz * seqlen, D_MODEL)
    for layer in range(depth):
        proj = _inproj(h, norm_mix[layer].reshape(1, -1), _arrange_w_in(w_in[layer]))
        attn = _moba(proj, cosf, sinf, bsz, seqlen)
        ssm = _ssd(proj, ssm_conv_w[layer], ssm_conv_b[layer], ssm_dt_bias[layer], ssm_a_log[layer],
                   ssm_d[layer], ssm_norm_w[layer], bsz, seqlen)
        conv = _conformer(proj, cf_conv_w[layer], cf_conv_b[layer], cf_ln_w[layer], cf_ln_b[layer], bsz, seqlen)
        h = _outproj(attn, ssm, conv, w_out[layer].astype(BF16), h)
        nw = norm_ffn[layer].reshape(1, -1)
        if layer % 2 == 0:
            i = layer // 2
            h = _ffn(h, nw, ffn_w_gate[i].astype(BF16), ffn_w_up[i].astype(BF16), ffn_w_down[i].astype(BF16))
        else:
            i = layer // 2
            hn, comb, rank, rankt, cnt_after = _router(h, nw, moe_router[i])
            plan = _moe_plan(cnt_after, h.shape[0])
            xs, gs = _dispatch(plan, hn, comb, rankt)
            ys = _moe(plan, xs, gs, moe_w_gate[i], moe_w_up[i], moe_w_down[i])
            h = _combine(plan, h, rank, ys, norm_final.reshape(1, -1))
    return h.reshape(bsz, seqlen, D_MODEL)
```
